```python
import math
import jax, jax.numpy as jnp
from jax import lax
import numpy as np

D_MODEL = 1024
BATCH = 4
SEQ = 8192
DEPTH = 1

S5_WIDTH = D_MODEL // 2
S5_GROUP = 16
S5_GROUPS = S5_WIDTH // S5_GROUP
S5_STATE = 64
DT_MIN = 1e-3
DT_MAX = 1e-1
RET_HEADS = 4
RET_DK = D_MODEL // 8
RET_DV = 2 * RET_DK
RET_QK_WIDTH = RET_HEADS * RET_DK
RET_V_WIDTH = RET_HEADS * RET_DV
RET_CHUNK = 128
ROPE_BASE = 10000.0
N_EXPERTS = 32
TOP_K = 4
D_FF = D_MODEL
SWIGLU_ALPHA = 1.702
SWIGLU_LIMIT = 7.0
MOE_BLOCK = 128
NORM_EPS = 1e-5
IN_WIDTH = S5_WIDTH + 2 * RET_QK_WIDTH + 2 * RET_V_WIDTH + 2 * D_MODEL
SPLIT_POINTS = (
    S5_WIDTH,
    S5_WIDTH + RET_QK_WIDTH,
    S5_WIDTH + 2 * RET_QK_WIDTH,
    S5_WIDTH + 2 * RET_QK_WIDTH + RET_V_WIDTH,
    S5_WIDTH + 2 * RET_QK_WIDTH + 2 * RET_V_WIDTH,
    S5_WIDTH + 2 * RET_QK_WIDTH + 2 * RET_V_WIDTH + D_MODEL,
)

kernel_name = "hybrid_s5_retention_gated_moe"

F32 = jnp.float32


def rms_norm(x, g):
    xf = x.astype(F32)
    y = xf * lax.rsqrt(jnp.mean(xf * xf, axis=-1, keepdims=True) + NORM_EPS)
    return (y * g.astype(F32)).astype(x.dtype)


def s5_mixer(u, lam_re, lam_im, log_dt, b_re, b_im, c_re, c_im, d_skip, w_glu, b_glu):
    bsz, seqlen, _ = u.shape
    dt = jnp.exp(log_dt.astype(F32))[:, None]
    lr, li = lam_re.astype(F32), lam_im.astype(F32)
    mag = jnp.exp(lr * dt)
    abar_re, abar_im = mag * jnp.cos(li * dt), mag * jnp.sin(li * dt)
    nr, ni = abar_re - 1.0, abar_im
    den = lr * lr + li * li
    f_re = (nr * lr + ni * li) / den
    f_im = (ni * lr - nr * li) / den
    ug = jnp.swapaxes(u, 0, 1).reshape(seqlen, bsz, S5_GROUPS, S5_GROUP).astype(F32)
    bu_re = jnp.einsum('lbgc,gnc->lbgn', ug, b_re.astype(F32))
    bu_im = jnp.einsum('lbgc,gnc->lbgn', ug, b_im.astype(F32))
    x_re = f_re * bu_re - f_im * bu_im
    x_im = f_re * bu_im + f_im * bu_re
    a_re = jnp.broadcast_to(abar_re[None, None], (seqlen, 1, S5_GROUPS, S5_STATE))
    a_im = jnp.broadcast_to(abar_im[None, None], (seqlen, 1, S5_GROUPS, S5_STATE))

    def combine(e1, e2):
        a1r, a1i, b1r, b1i = e1
        a2r, a2i, b2r, b2i = e2
        return (a2r * a1r - a2i * a1i,
                a2r * a1i + a2i * a1r,
                a2r * b1r - a2i * b1i + b2r,
                a2r * b1i + a2i * b1r + b2i)

    _, _, s_re, s_im = lax.associative_scan(combine, (a_re, a_im, x_re, x_im), axis=0)
    y = (jnp.einsum('lbgn,gcn->lbgc', s_re, c_re.astype(F32))
         - jnp.einsum('lbgn,gcn->lbgc', s_im, c_im.astype(F32)))
    y = jnp.swapaxes(y.reshape(seqlen, bsz, S5_WIDTH), 0, 1).astype(u.dtype) + d_skip * u
    z = jax.nn.gelu(y)
    return z * jax.nn.sigmoid(z @ w_glu + b_glu)


def rotary(x, positions):
    half = x.shape[-1] // 2
    inv_freq = ROPE_BASE ** (-jnp.arange(half, dtype=F32) / half)
    ang = positions.astype(F32)[..., None, None] * inv_freq
    cos, sin = jnp.cos(ang), jnp.sin(ang)
    x1, x2 = x[..., :half].astype(F32), x[..., half:].astype(F32)
    return jnp.concatenate([x1 * cos - x2 * sin, x2 * cos + x1 * sin], axis=-1).astype(x.dtype)


def retention(q, k, v):
    bsz, seqlen = q.shape[:2]
    n_chunks = seqlen // RET_CHUNK
    dt = q.dtype
    log_g = jnp.log1p(-(2.0 ** (-5.0 - jnp.arange(RET_HEADS, dtype=F32))))
    pos = jnp.arange(RET_CHUNK, dtype=F32)
    rel = pos[:, None] - pos[None, :]
    intra = jnp.where(rel >= 0, jnp.exp(log_g[:, None, None] * jnp.maximum(rel, 0.0)), 0.0).astype(dt)
    q_decay = jnp.exp(log_g[:, None] * (pos + 1.0)).astype(dt)
    k_decay = jnp.exp(log_g[:, None] * (RET_CHUNK - 1.0 - pos)).astype(dt)
    chunk_decay = jnp.exp(log_g * RET_CHUNK).astype(dt)

    def to_chunks(t):
        return t.reshape(bsz, n_chunks, RET_CHUNK, RET_HEADS, t.shape[-1]).transpose(1, 0, 3, 2, 4)

    def step(state, chunk):
        qc, kc, vc = chunk
        scores = jnp.einsum('bhid,bhjd->bhij', qc, kc) * intra
        inner = jnp.einsum('bhij,bhjv->bhiv', scores, vc)
        cross = jnp.einsum('bhid,bhdv->bhiv', qc * q_decay[:, :, None], state)
        state = chunk_decay[:, None, None] * state + jnp.einsum('bhjd,bhjv->bhdv', kc * k_decay[:, :, None], vc)
        return state, inner + cross

    state0 = jnp.zeros((bsz, RET_HEADS, RET_DK, RET_DV), dt)
    _, out = lax.scan(step, state0, (to_chunks(q), to_chunks(k), to_chunks(v)))
    return out.transpose(1, 0, 3, 2, 4).reshape(bsz, seqlen, RET_HEADS, RET_DV)


def head_group_norm(y, g):
    yf = y.astype(F32)
    mu = jnp.mean(yf, axis=-1, keepdims=True)
    var = jnp.mean(jnp.square(yf - mu), axis=-1, keepdims=True)
    yn = (yf - mu) * lax.rsqrt(var + NORM_EPS)
    yn = yn.reshape(y.shape[0], y.shape[1], RET_V_WIDTH) * g.astype(F32)
    return yn.astype(y.dtype)


def clamped_swiglu(gu):
    x_glu = jnp.minimum(gu[..., ::2], SWIGLU_LIMIT)
    x_lin = jnp.clip(gu[..., 1::2], -SWIGLU_LIMIT, SWIGLU_LIMIT)
    return x_glu * jax.nn.sigmoid(SWIGLU_ALPHA * x_glu) * (x_lin + 1.0)


def moe_ffn(h, router_w, router_b, w1, b1, w2, b2):
    bsz, seqlen, d = h.shape
    n_tok = bsz * seqlen
    ht = h.reshape(n_tok, d)
    logits = ht.astype(F32) @ router_w.astype(F32) + router_b.astype(F32)
    top_vals, top_idx = lax.top_k(logits, TOP_K)
    gates = jax.nn.softmax(top_vals, axis=-1)
    n_assign = n_tok * TOP_K
    flat_e = top_idx.reshape(n_assign)
    flat_tok = jnp.arange(n_assign, dtype=jnp.int32) // TOP_K
    order = jnp.argsort(flat_e)
    e_sorted = flat_e[order]
    tok_sorted = flat_tok[order]
    gate_sorted = gates.reshape(n_assign)[order]
    counts = jnp.bincount(flat_e, length=N_EXPERTS)
    padded = (counts + MOE_BLOCK - 1) // MOE_BLOCK * MOE_BLOCK
    start = jnp.cumsum(counts) - counts
    pend = jnp.cumsum(padded)
    pstart = pend - padded
    dest = pstart[e_sorted] + jnp.arange(n_assign, dtype=jnp.int32) - start[e_sorted]
    n_blocks = (n_assign + MOE_BLOCK - 1) // MOE_BLOCK + N_EXPERTS
    n_rows = n_blocks * MOE_BLOCK
    x_rows = jnp.zeros((n_rows, d), h.dtype).at[dest].set(ht[tok_sorted])
    block_start = jnp.arange(n_blocks, dtype=pend.dtype) * MOE_BLOCK
    block_expert = jnp.minimum(jnp.searchsorted(pend, block_start, side='right'), N_EXPERTS - 1)

    def expert_block(args):
        xb, e = args
        gu = xb @ w1[e] + b1[e]
        return clamped_swiglu(gu) @ w2[e] + b2[e]

    y_rows = lax.map(expert_block, (x_rows.reshape(n_blocks, MOE_BLOCK, d), block_expert))
    y_rows = y_rows.reshape(n_rows, d)
    y = jax.ops.segment_sum(y_rows[dest] * gate_sorted[:, None].astype(h.dtype), tok_sorted,
                            num_segments=n_tok)
    return y.reshape(bsz, seqlen, d)


def setup_inputs(seed: int = 0) -> dict:
    key = jax.random.key(seed)
    ks = jax.random.split(key, 24)
    L = DEPTH
    G, N, W = S5_GROUPS, S5_STATE, S5_WIDTH

    def nrm(k, shape, scale):
        return jax.random.normal(k, shape, F32) * scale

    x = nrm(ks[0], (BATCH, SEQ, D_MODEL), 1.0)
    positions = jnp.broadcast_to(jnp.arange(SEQ, dtype=jnp.int32)[None, :], (BATCH, SEQ))
    norm_mix_g = 1.0 + nrm(ks[1], (L, D_MODEL), 0.02)
    w_in = nrm(ks[2], (L, D_MODEL, IN_WIDTH), D_MODEL ** -0.5)
    s5_lambda_re = -0.5 + nrm(ks[3], (L, G, N), 0.01)
    s5_lambda_im = math.pi * jnp.arange(N, dtype=F32) + nrm(ks[4], (L, G, N), 0.01)
    s5_log_dt = jax.random.uniform(ks[5], (L, G), F32, math.log(DT_MIN), math.log(DT_MAX))
    s5_b_re = nrm(ks[6], (L, G, N, S5_GROUP), (2 * S5_GROUP) ** -0.5)
    s5_b_im = nrm(ks[7], (L, G, N, S5_GROUP), (2 * S5_GROUP) ** -0.5)
    s5_c_re = nrm(ks[8], (L, G, S5_GROUP, N), (2 * N) ** -0.5)
    s5_c_im = nrm(ks[9], (L, G, S5_GROUP, N), (2 * N) ** -0.5)
    s5_d = nrm(ks[10], (L, W), 1.0)
    s5_w_glu = nrm(ks[11], (L, W, W), W ** -0.5)
    s5_b_glu = nrm(ks[12], (L, W), 0.01)
    ret_gn_g = 1.0 + nrm(ks[13], (L, RET_V_WIDTH), 0.02)
    w_branch_s5 = nrm(ks[14], (L, W, D_MODEL), W ** -0.5)
    w_branch_ret = nrm(ks[15], (L, RET_V_WIDTH, D_MODEL), RET_V_WIDTH ** -0.5)
    w_out = nrm(ks[16], (L, D_MODEL, D_MODEL), D_MODEL ** -0.5)
    norm_ffn_g = 1.0 + nrm(ks[17], (L, D_MODEL), 0.02)
    router_w = nrm(ks[18], (L, D_MODEL, N_EXPERTS), D_MODEL ** -0.5)
    router_b = nrm(ks[19], (L, N_EXPERTS), 0.01)
    expert_w1 = nrm(ks[20], (L, N_EXPERTS, D_MODEL, 2 * D_FF), D_MODEL ** -0.5)
    expert_b1 = nrm(ks[21], (L, N_EXPERTS, 2 * D_FF), 0.01)
    expert_w2 = nrm(ks[22], (L, N_EXPERTS, D_FF, D_MODEL), D_FF ** -0.5)
    k23a, k23b = jax.random.split(ks[23])
    expert_b2 = nrm(k23a, (L, N_EXPERTS, D_MODEL), 0.01)
    norm_final_g = 1.0 + nrm(k23b, (D_MODEL,), 0.02)
    return {"x": x, "positions": positions, "norm_mix_g": norm_mix_g, "w_in": w_in,
            "s5_lambda_re": s5_lambda_re, "s5_lambda_im": s5_lambda_im, "s5_log_dt": s5_log_dt,
            "s5_b_re": s5_b_re, "s5_b_im": s5_b_im, "s5_c_re": s5_c_re, "s5_c_im": s5_c_im,
            "s5_d": s5_d, "s5_w_glu": s5_w_glu, "s5_b_glu": s5_b_glu, "ret_gn_g": ret_gn_g,
            "w_branch_s5": w_branch_s5, "w_branch_ret": w_branch_ret, "w_out": w_out,
            "norm_ffn_g": norm_ffn_g, "router_w": router_w, "router_b": router_b,
            "expert_w1": expert_w1, "expert_b1": expert_b1, "expert_w2": expert_w2,
            "expert_b2": expert_b2, "norm_final_g": norm_final_g}


def reference(x, positions, norm_mix_g, w_in, s5_lambda_re, s5_lambda_im, s5_log_dt,
              s5_b_re, s5_b_im, s5_c_re, s5_c_im, s5_d, s5_w_glu, s5_b_glu, ret_gn_g,
              w_branch_s5, w_branch_ret, w_out, norm_ffn_g, router_w, router_b,
              expert_w1, expert_b1, expert_w2, expert_b2, norm_final_g):
    bsz, seqlen, _ = x.shape
    for layer in range(DEPTH):
        h = rms_norm(x, norm_mix_g[layer])
        proj = h @ w_in[layer]
        u_s5, q, k, v, g_ret, gate_a, gate_b = jnp.split(proj, SPLIT_POINTS, axis=-1)
        y_s5 = s5_mixer(u_s5, s5_lambda_re[layer], s5_lambda_im[layer], s5_log_dt[layer],
                        s5_b_re[layer], s5_b_im[layer], s5_c_re[layer], s5_c_im[layer],
                        s5_d[layer], s5_w_glu[layer], s5_b_glu[layer])
        y_a = y_s5 @ w_branch_s5[layer]
        q = rotary(q.reshape(bsz, seqlen, RET_HEADS, RET_DK), positions)
        k = rotary(k.reshape(bsz, seqlen, RET_HEADS, RET_DK), positions) * (RET_DK ** -0.5)
        v = v.reshape(bsz, seqlen, RET_HEADS, RET_DV)
        ret = head_group_norm(retention(q, k, v), ret_gn_g[layer])
        y_b = (jax.nn.silu(g_ret) * ret) @ w_branch_ret[layer]
        merged = jax.nn.sigmoid(gate_a) * y_a + jax.nn.sigmoid(gate_b) * y_b
        x = x + merged @ w_out[layer]
        h2 = rms_norm(x, norm_ffn_g[layer])
        x = x + moe_ffn(h2, router_w[layer], router_b[layer], expert_w1[layer],
                        expert_b1[layer], expert_w2[layer], expert_b2[layer])
    return rms_norm(x, norm_final_g)
```

```python
import functools
import math

import jax
import jax.numpy as jnp
from jax import lax
from jax.experimental import pallas as pl
from jax.experimental.pallas import tpu as pltpu

F32 = jnp.float32
BF16 = jnp.bfloat16

D_MODEL = 1024
S5_WIDTH = 512
S5_GROUP = 16
S5_GROUPS = 32
S5_STATE = 64
S5_LANES = S5_GROUPS * S5_STATE
RET_HEADS = 4
RET_DK = 128
RET_DV = 256
RET_QK_WIDTH = RET_HEADS * RET_DK
RET_V_WIDTH = RET_HEADS * RET_DV
RET_CHUNK = 128
ROPE_BASE = 10000.0
N_EXPERTS = 32
TOP_K = 4
D_FF = 1024
SWIGLU_ALPHA = 1.702
SWIGLU_LIMIT = 7.0
NORM_EPS = 1e-5
IN_WIDTH = S5_WIDTH + 2 * RET_QK_WIDTH + 2 * RET_V_WIDTH + 2 * D_MODEL
OFF_Q = S5_WIDTH
OFF_V = OFF_Q + 2 * RET_QK_WIDTH
OFF_G = OFF_V + RET_V_WIDTH
OFF_GA = OFF_G + RET_V_WIDTH
OFF_GB = OFF_GA + D_MODEL

LANES = 128
SUBLANES = 8
VMEM_LIMIT = 56 * 1024 * 1024

TM_PROJ = 512
TT_S5 = 128
S5_LC = 512
TR_RET = 512
BM_EXPERT = 256
TD_ROWS = 256


def _rms(x, g):
    return x * lax.rsqrt(jnp.mean(x * x, axis=-1, keepdims=True) + NORM_EPS) * g


def _inproj_body(x_ref, pos_ref, rope_ref, g_ref, w_ref,
                 u_ref, q_ref, k_ref, v_ref, gr_ref, ga_ref, gb_ref):
    hb = _rms(x_ref[0], g_ref[...]).astype(BF16)

    def proj(lo, width):
        return jnp.dot(hb, w_ref[:, lo:lo + width], preferred_element_type=F32)

    u_ref[...] = proj(0, S5_WIDTH).astype(BF16)
    ang = pos_ref[0] * rope_ref[0:1, :]
    cos = jnp.cos(ang)
    sin = jnp.sin(ang) * rope_ref[1:2, :]
    qk = proj(OFF_Q, 2 * RET_QK_WIDTH)
    for h in range(RET_HEADS):
        qh = qk[:, h * RET_DK:(h + 1) * RET_DK]
        q_ref[0, :, h * RET_DK:(h + 1) * RET_DK] = (
            qh * cos + pltpu.roll(qh, RET_DK // 2, 1) * sin).astype(BF16)
        kh = qk[:, RET_QK_WIDTH + h * RET_DK:RET_QK_WIDTH + (h + 1) * RET_DK]
        k_ref[0, :, h * RET_DK:(h + 1) * RET_DK] = (
            (kh * cos + pltpu.roll(kh, RET_DK // 2, 1) * sin) * (RET_DK ** -0.5)).astype(BF16)
    v_ref[0] = proj(OFF_V, RET_V_WIDTH).astype(BF16)
    gr_ref[0] = proj(OFF_G, RET_V_WIDTH).astype(BF16)
    ga_ref[0] = proj(OFF_GA, D_MODEL).astype(BF16)
    gb_ref[0] = proj(OFF_GB, D_MODEL).astype(BF16)


def _inproj(x, pos128, rope, g, w_bf):
    bsz, seqlen, _ = x.shape
    tm = min(TM_PROJ, seqlen)
    grid = (bsz, seqlen // tm)
    row = lambda b, t: (b, t, 0)
    const = lambda b, t: (0, 0)
    out_shape = (
        jax.ShapeDtypeStruct((seqlen, bsz * S5_WIDTH), BF16),
        jax.ShapeDtypeStruct((bsz, seqlen, RET_QK_WIDTH), BF16),
        jax.ShapeDtypeStruct((bsz, seqlen, RET_QK_WIDTH), BF16),
        jax.ShapeDtypeStruct((bsz, seqlen, RET_V_WIDTH), BF16),
        jax.ShapeDtypeStruct((bsz, seqlen, RET_V_WIDTH), BF16),
        jax.ShapeDtypeStruct((bsz, seqlen, D_MODEL), BF16),
        jax.ShapeDtypeStruct((bsz, seqlen, D_MODEL), BF16),
    )
    return pl.pallas_call(
        _inproj_body,
        grid=grid,
        in_specs=[
            pl.BlockSpec((1, tm, D_MODEL), row),
            pl.BlockSpec((1, tm, LANES), row),
            pl.BlockSpec((2, LANES), const),
            pl.BlockSpec((1, D_MODEL), const),
            pl.BlockSpec((D_MODEL, IN_WIDTH), const, pipeline_mode=pl.Buffered(1)),
        ],
        out_specs=(
            pl.BlockSpec((tm, S5_WIDTH), lambda b, t: (t, b)),
            pl.BlockSpec((1, tm, RET_QK_WIDTH), row),
            pl.BlockSpec((1, tm, RET_QK_WIDTH), row),
            pl.BlockSpec((1, tm, RET_V_WIDTH), row),
            pl.BlockSpec((1, tm, RET_V_WIDTH), row),
            pl.BlockSpec((1, tm, D_MODEL), row),
            pl.BlockSpec((1, tm, D_MODEL), row),
        ),
        out_shape=out_shape,
        compiler_params=pltpu.CompilerParams(
            dimension_semantics=("arbitrary", "arbitrary"), vmem_limit_bytes=VMEM_LIMIT),
        name="inproj",
    )(x, pos128, rope, g, w_bf)


def _s5_body(u_ref, lam_ref, bblk_ref, cblk_ref, dvec_ref, wglu_ref, o_ref,
             bf_ref, a8_ref, p_ref, xre_ref, xim_ref):
    rows = u_ref.shape[0]

    @pl.when(pl.program_id(0) == 0)
    def _init():
        lr = lam_ref[0:1, :]
        li = lam_ref[1:2, :]
        dt = jnp.exp(lam_ref[2:3, :])
        mag = jnp.exp(lr * dt)
        a_re = mag * jnp.cos(li * dt)
        a_im = mag * jnp.sin(li * dt)
        nr = a_re - 1.0
        den = lr * lr + li * li
        f_re = (nr * lr + a_im * li) / den
        f_im = (a_im * lr - nr * li) / den
        b_re = bblk_ref[0]
        b_im = bblk_ref[1]
        bf_ref[0] = (b_re * f_re - b_im * f_im).astype(BF16)
        bf_ref[1] = (b_im * f_re + b_re * f_im).astype(BF16)
        a8_ref[0] = jnp.broadcast_to(a_re, (SUBLANES, S5_LANES))
        a8_ref[1] = jnp.broadcast_to(a_im, (SUBLANES, S5_LANES))
        p_ref[...] = jnp.zeros_like(p_ref)

    ub = u_ref[...]
    xre_ref[...] = jnp.dot(ub, bf_ref[0], preferred_element_type=F32)
    xim_ref[...] = jnp.dot(ub, bf_ref[1], preferred_element_type=F32)

    lower = lax.broadcasted_iota(jnp.int32, (SUBLANES, S5_LC), 0) < (SUBLANES // 2)
    for j in range(S5_LANES // S5_LC):
        sl = slice(j * S5_LC, (j + 1) * S5_LC)
        ar = a8_ref[0, :, sl]
        ai = a8_ref[1, :, sl]

        def step(k, carry, sl=sl, ar=ar, ai=ai):
            pr, pi = carry
            r0 = pl.multiple_of(k * SUBLANES, SUBLANES)
            xr = xre_ref[pl.ds(r0, SUBLANES), sl]
            xi = xim_ref[pl.ds(r0, SUBLANES), sl]
            s1r = ar * pr - ai * pi + xr
            s1i = ar * pi + ai * pr + xi
            tr = pltpu.roll(s1r, SUBLANES // 2, 0)
            ti = pltpu.roll(s1i, SUBLANES // 2, 0)
            s2r = ar * tr - ai * ti + xr
            s2i = ar * ti + ai * tr + xi
            xre_ref[pl.ds(r0, SUBLANES), sl] = jnp.where(lower, s1r, s2r)
            xim_ref[pl.ds(r0, SUBLANES), sl] = jnp.where(lower, s1i, s2i)
            return pltpu.roll(s2r, SUBLANES // 2, 0), pltpu.roll(s2i, SUBLANES // 2, 0)

        pr, pi = lax.fori_loop(0, rows // SUBLANES, step, (p_ref[0, :, sl], p_ref[1, :, sl]),
                               unroll=2)
        p_ref[0, :, sl] = pr
        p_ref[1, :, sl] = pi

    y = (jnp.dot(xre_ref[...].astype(BF16), cblk_ref[0], preferred_element_type=F32)
         - jnp.dot(xim_ref[...].astype(BF16), cblk_ref[1], preferred_element_type=F32))
    y = y + dvec_ref[0:1, :] * ub.astype(F32)
    z = jax.nn.gelu(y)
    gate = jax.nn.sigmoid(
        jnp.dot(z.astype(BF16), wglu_ref[...], preferred_element_type=F32) + dvec_ref[1:2, :])
    o_ref[...] = (z * gate).astype(BF16)


def _s5(u2, lam, bblk, cblk, dvec, wglu, bsz):
    n_rows = u2.shape[0]
    rows = min(TT_S5 * bsz, n_rows)
    const2 = lambda i: (0, 0)
    const3 = lambda i: (0, 0, 0)
    return pl.pallas_call(
        _s5_body,
        grid=(n_rows // rows,),
        in_specs=[
            pl.BlockSpec((rows, S5_WIDTH), lambda i: (i, 0)),
            pl.BlockSpec((3, S5_LANES), const2),
            pl.BlockSpec((2, S5_WIDTH, S5_LANES), const3),
            pl.BlockSpec((2, S5_LANES, S5_WIDTH), const3),
            pl.BlockSpec((2, S5_WIDTH), const2),
            pl.BlockSpec((S5_WIDTH, S5_WIDTH), const2),
        ],
        out_specs=pl.BlockSpec((rows, S5_WIDTH), lambda i: (i, 0)),
        out_shape=jax.ShapeDtypeStruct((n_rows, S5_WIDTH), BF16),
        scratch_shapes=[
            pltpu.VMEM((2, S5_WIDTH, S5_LANES), BF16),
            pltpu.VMEM((2, SUBLANES, S5_LANES), F32),
            pltpu.VMEM((2, SUBLANES, S5_LANES), F32),
            pltpu.VMEM((rows, S5_LANES), F32),
            pltpu.VMEM((rows, S5_LANES), F32),
        ],
        compiler_params=pltpu.CompilerParams(
            dimension_semantics=("arbitrary",), vmem_limit_bytes=VMEM_LIMIT),
        name="s5_mixer",
    )(u2, lam, bblk, cblk, dvec, wglu)


def _ret_body(q_ref, k_ref, v_ref, g_ref, intra_ref, qd_ref, kd_ref, cd_ref, gn_ref, o_ref,
              st_ref):
    tr = q_ref.shape[1]

    @pl.when(pl.program_id(1) == 0)
    def _init():
        st_ref[...] = jnp.zeros_like(st_ref)

    for h in range(RET_HEADS):
        qs = slice(h * RET_DK, (h + 1) * RET_DK)
        vs = slice(h * RET_DV, (h + 1) * RET_DV)
        for c in range(tr // RET_CHUNK):
            rs = slice(c * RET_CHUNK, (c + 1) * RET_CHUNK)
            qc = q_ref[0, rs, qs]
            kc = k_ref[0, rs, qs]
            vc = v_ref[0, rs, vs]
            scores = lax.dot_general(qc, kc, (((1,), (1,)), ((), ())),
                                     preferred_element_type=F32) * intra_ref[h]
            inner = jnp.dot(scores.astype(BF16), vc, preferred_element_type=F32)
            st = st_ref[h]
            qdec = (qc.astype(F32) * qd_ref[h]).astype(BF16)
            cross = jnp.dot(qdec, st.astype(BF16), preferred_element_type=F32)
            kdec_t = (kc.astype(F32) * kd_ref[h]).T.astype(BF16)
            st_ref[h] = cd_ref[h] * st + jnp.dot(kdec_t, vc, preferred_element_type=F32)
            ret = inner + cross
            mu = jnp.mean(ret, axis=-1, keepdims=True)
            dev = ret - mu
            var = jnp.mean(dev * dev, axis=-1, keepdims=True)
            yn = dev * lax.rsqrt(var + NORM_EPS) * gn_ref[0:1, vs]
            o_ref[0, rs, vs] = (jax.nn.silu(g_ref[0, rs, vs].astype(F32)) * yn).astype(BF16)


def _retention(q, k, v, g, intra, qd, kd, cd, gn):
    bsz, seqlen, _ = q.shape
    tr = min(TR_RET, seqlen)
    row = lambda b, t: (b, t, 0)
    const3 = lambda b, t: (0, 0, 0)
    return pl.pallas_call(
        _ret_body,
        grid=(bsz, seqlen // tr),
        in_specs=[
            pl.BlockSpec((1, tr, RET_QK_WIDTH), row),
            pl.BlockSpec((1, tr, RET_QK_WIDTH), row),
            pl.BlockSpec((1, tr, RET_V_WIDTH), row),
            pl.BlockSpec((1, tr, RET_V_WIDTH), row),
            pl.BlockSpec((RET_HEADS, RET_CHUNK, RET_CHUNK), const3),
            pl.BlockSpec((RET_HEADS, RET_CHUNK, RET_DK), const3),
            pl.BlockSpec((RET_HEADS, RET_CHUNK, RET_DK), const3),
            pl.BlockSpec((RET_HEADS, 1, RET_DV), const3),
            pl.BlockSpec((1, RET_V_WIDTH), lambda b, t: (0, 0)),
        ],
        out_specs=pl.BlockSpec((1, tr, RET_V_WIDTH), row),
        out_shape=jax.ShapeDtypeStruct((bsz, seqlen, RET_V_WIDTH), BF16),
        scratch_shapes=[pltpu.VMEM((RET_HEADS, RET_DK, RET_DV), F32)],
        compiler_params=pltpu.CompilerParams(
            dimension_semantics=("arbitrary", "arbitrary"), vmem_limit_bytes=VMEM_LIMIT),
        name="retention",
    )(q, k, v, g, intra, qd, kd, cd, gn)


def _merge_body(x_ref, ys_ref, rg_ref, ga_ref, gb_ref, wbs_ref, wbr_ref, wo_ref, gf_ref,
                rw_ref, rb_ref,
                x1_ref, h2_ref, ri_ref, gt_ref, cnt_ref, run_ref):
    tm = x_ref.shape[1]

    @pl.when((pl.program_id(0) == 0) & (pl.program_id(1) == 0))
    def _init():
        run_ref[...] = jnp.zeros_like(run_ref)

    y_a = jnp.dot(ys_ref[...], wbs_ref[...], preferred_element_type=F32)
    y_b = jnp.dot(rg_ref[0], wbr_ref[...], preferred_element_type=F32)
    merged = (jax.nn.sigmoid(ga_ref[0].astype(F32)) * y_a
              + jax.nn.sigmoid(gb_ref[0].astype(F32)) * y_b)
    x1 = x_ref[0] + jnp.dot(merged.astype(BF16), wo_ref[...], preferred_element_type=F32)
    x1_ref[0] = x1
    h2 = _rms(x1, gf_ref[...])
    h2_ref[0] = h2

    lane = lax.broadcasted_iota(jnp.int32, (tm, LANES), 1)
    logits = jnp.dot(h2, rw_ref[...], preferred_element_type=F32,
                     precision=lax.Precision.HIGHEST) + rb_ref[...]
    work = jnp.where(lane < N_EXPERTS, logits, -jnp.inf)
    lane_f = lane.astype(F32)
    vals, hots = [], []
    ri = jnp.zeros((tm, LANES), jnp.int32)
    for kk in range(TOP_K):
        m = jnp.max(work, axis=-1, keepdims=True)
        idx = jnp.min(jnp.where(work == m, lane_f, float(LANES)), axis=-1, keepdims=True)
        hot = lane_f == idx
        vals.append(m)
        hots.append(hot)
        ri = jnp.where(lane == kk, idx.astype(jnp.int32), ri)
        work = jnp.where(hot, -jnp.inf, work)
    exps = [jnp.exp(v - vals[0]) for v in vals]
    denom = exps[0] + exps[1] + exps[2] + exps[3]
    gt = jnp.zeros((tm, LANES), F32)
    for kk in range(TOP_K):
        gt = jnp.where(lane == kk, exps[kk] / denom, gt)
    gt_ref[0] = gt

    onehot = [h.astype(F32) for h in hots]
    tot = onehot[0] + onehot[1] + onehot[2] + onehot[3]
    r_i = lax.broadcasted_iota(jnp.int32, (tm, tm), 0)
    c_i = lax.broadcasted_iota(jnp.int32, (tm, tm), 1)
    tril = (c_i < r_i).astype(BF16)
    base = jnp.dot(tril, tot.astype(BF16), preferred_element_type=F32) + run_ref[...]
    for kk in range(TOP_K):
        rank = jnp.sum(onehot[kk] * base, axis=-1, keepdims=True).astype(jnp.int32)
        ri = jnp.where(lane == TOP_K + kk, rank, ri)
    ri_ref[0] = ri
    run_ref[...] = run_ref[...] + jnp.sum(tot, axis=0, keepdims=True)
    cnt_ref[...] = run_ref[...]


def _merge(x, ys5, retg, ga, gb, wbs, wbr, wo, gf, rw, rb):
    bsz, seqlen, _ = x.shape
    tm = min(TM_PROJ, seqlen)
    row = lambda b, t: (b, t, 0)
    const = lambda b, t: (0, 0)
    out_shape = (
        jax.ShapeDtypeStruct((bsz, seqlen, D_MODEL), F32),
        jax.ShapeDtypeStruct((bsz, seqlen, D_MODEL), F32),
        jax.ShapeDtypeStruct((bsz, seqlen, LANES), jnp.int32),
        jax.ShapeDtypeStruct((bsz, seqlen, LANES), F32),
        jax.ShapeDtypeStruct((1, LANES), F32),
    )
    return pl.pallas_call(
        _merge_body,
        grid=(bsz, seqlen // tm),
        in_specs=[
            pl.BlockSpec((1, tm, D_MODEL), row),
            pl.BlockSpec((tm, S5_WIDTH), lambda b, t: (t, b)),
            pl.BlockSpec((1, tm, RET_V_WIDTH), row),
            pl.BlockSpec((1, tm, D_MODEL), row),
            pl.BlockSpec((1, tm, D_MODEL), row),
            pl.BlockSpec((S5_WIDTH, D_MODEL), const),
            pl.BlockSpec((RET_V_WIDTH, D_MODEL), const),
            pl.BlockSpec((D_MODEL, D_MODEL), const),
            pl.BlockSpec((1, D_MODEL), const),
            pl.BlockSpec((D_MODEL, LANES), const),
            pl.BlockSpec((1, LANES), const),
        ],
        out_specs=(
            pl.BlockSpec((1, tm, D_MODEL), row),
            pl.BlockSpec((1, tm, D_MODEL), row),
            pl.BlockSpec((1, tm, LANES), row),
            pl.BlockSpec((1, tm, LANES), row),
            pl.BlockSpec((1, LANES), const),
        ),
        out_shape=out_shape,
        scratch_shapes=[pltpu.VMEM((1, LANES), F32)],
        compiler_params=pltpu.CompilerParams(
            dimension_semantics=("arbitrary", "arbitrary"), vmem_limit_bytes=VMEM_LIMIT),
        name="merge_router",
    )(x, ys5, retg, ga, gb, wbs, wbr, wo, gf, rw, rb)


def _dispatch_body(dest_ref, h_ref, xin_ref, xout_ref, sem):
    del xin_ref
    td = h_ref.shape[0]

    def copy(r, kk):
        d = dest_ref[0, 0, r * TOP_K + kk]
        return pltpu.make_async_copy(h_ref.at[pl.ds(r, 1)], xout_ref.at[pl.ds(d, 1)], sem)

    def start(r, _):
        for kk in range(TOP_K):
            copy(r, kk).start()
        return 0

    def wait(r, _):
        for kk in range(TOP_K):
            copy(r, kk).wait()
        return 0

    lax.fori_loop(0, td, start, 0)
    lax.fori_loop(0, td, wait, 0)


def _dispatch(dest, h2, x_rows_init):
    n_tok = h2.shape[0]
    td = min(TD_ROWS, n_tok)
    dest3 = dest.reshape(n_tok // td, 1, td * TOP_K)
    return pl.pallas_call(
        _dispatch_body,
        grid=(n_tok // td,),
        in_specs=[
            pl.BlockSpec((1, 1, td * TOP_K), lambda i: (i, 0, 0), memory_space=pltpu.SMEM),
            pl.BlockSpec((td, D_MODEL), lambda i: (i, 0)),
            pl.BlockSpec(memory_space=pl.ANY),
        ],
        out_specs=pl.BlockSpec(memory_space=pl.ANY),
        out_shape=jax.ShapeDtypeStruct(x_rows_init.shape, x_rows_init.dtype),
        scratch_shapes=[pltpu.SemaphoreType.DMA(())],
        input_output_aliases={2: 0},
        compiler_params=pltpu.CompilerParams(
            dimension_semantics=("arbitrary",), has_side_effects=True),
        name="dispatch",
    )(dest3, h2, x_rows_init)


def _expert_body(be_ref, x_ref, w1_ref, b1_ref, w2_ref, b2_ref, y_ref):
    del be_ref
    xb = x_ref[...].astype(BF16)
    gu = jnp.dot(xb, w1_ref[0], preferred_element_type=F32) + b1_ref[0]
    x_glu = jnp.minimum(gu[:, :D_FF], SWIGLU_LIMIT)
    x_lin = jnp.clip(gu[:, D_FF:], -SWIGLU_LIMIT, SWIGLU_LIMIT)
    act = x_glu * jax.nn.sigmoid(SWIGLU_ALPHA * x_glu) * (x_lin + 1.0)
    y_ref[...] = jnp.dot(act.astype(BF16), w2_ref[0], preferred_element_type=F32) + b2_ref[0]


def _experts(block_expert, x_rows, w1, b1, w2, b2):
    n_rows = x_rows.shape[0]
    bm = BM_EXPERT
    grid_spec = pltpu.PrefetchScalarGridSpec(
        num_scalar_prefetch=1,
        grid=(n_rows // bm,),
        in_specs=[
            pl.BlockSpec((bm, D_MODEL), lambda i, be: (i, 0)),
            pl.BlockSpec((1, D_MODEL, 2 * D_FF), lambda i, be: (be[i], 0, 0)),
            pl.BlockSpec((1, 1, 2 * D_FF), lambda i, be: (be[i], 0, 0)),
            pl.BlockSpec((1, D_FF, D_MODEL), lambda i, be: (be[i], 0, 0)),
            pl.BlockSpec((1, 1, D_MODEL), lambda i, be: (be[i], 0, 0)),
        ],
        out_specs=pl.BlockSpec((bm, D_MODEL), lambda i, be: (i, 0)),
    )
    return pl.pallas_call(
        _expert_body,
        grid_spec=grid_spec,
        out_shape=jax.ShapeDtypeStruct((n_rows, D_MODEL), F32),
        compiler_params=pltpu.CompilerParams(
            dimension_semantics=("arbitrary",), vmem_limit_bytes=VMEM_LIMIT),
        name="expert_ffn",
    )(block_expert, x_rows, w1, b1, w2, b2)


def _combine_body(dest_ref, gt_ref, x1_ref, gfin_ref, yrows_ref, o_ref, ybuf_ref, sem):
    td = x1_ref.shape[0]

    def copy(r, kk):
        d = dest_ref[0, 0, r * TOP_K + kk]
        return pltpu.make_async_copy(yrows_ref.at[pl.ds(d, 1)], ybuf_ref.at[kk, pl.ds(r, 1)], sem)

    def start(r, _):
        for kk in range(TOP_K):
            copy(r, kk).start()
        return 0

    def wait(r, _):
        for kk in range(TOP_K):
            copy(r, kk).wait()
        return 0

    lax.fori_loop(0, td, start, 0)
    lax.fori_loop(0, td, wait, 0)
    gt = gt_ref[...]
    acc = x1_ref[...]
    for kk in range(TOP_K):
        acc = acc + gt[:, kk:kk + 1] * ybuf_ref[kk]
    o_ref[...] = _rms(acc, gfin_ref[...])


def _combine(dest, gates, x1, gfin, y_rows):
    n_tok = x1.shape[0]
    td = min(TD_ROWS, n_tok)
    dest3 = dest.reshape(n_tok // td, 1, td * TOP_K)
    return pl.pallas_call(
        _combine_body,
        grid=(n_tok // td,),
        in_specs=[
            pl.BlockSpec((1, 1, td * TOP_K), lambda i: (i, 0, 0), memory_space=pltpu.SMEM),
            pl.BlockSpec((td, LANES), lambda i: (i, 0)),
            pl.BlockSpec((td, D_MODEL), lambda i: (i, 0)),
            pl.BlockSpec((1, D_MODEL), lambda i: (0, 0)),
            pl.BlockSpec(memory_space=pl.ANY),
        ],
        out_specs=pl.BlockSpec((td, D_MODEL), lambda i: (i, 0)),
        out_shape=jax.ShapeDtypeStruct((n_tok, D_MODEL), F32),
        scratch_shapes=[pltpu.VMEM((TOP_K, td, D_MODEL), F32), pltpu.SemaphoreType.DMA(())],
        compiler_params=pltpu.CompilerParams(
            dimension_semantics=("arbitrary",), vmem_limit_bytes=VMEM_LIMIT),
        name="combine",
    )(dest3, gates, x1, gfin, y_rows)


def _block_diag(p):
    g, a, b = p.shape
    eye = jnp.eye(g, dtype=p.dtype)
    return (p[:, :, None, :] * eye[:, None, :, None]).reshape(g * a, g * b)


def _retention_tables():
    log_g = jnp.log1p(-(2.0 ** (-5.0 - jnp.arange(RET_HEADS, dtype=F32))))
    pos = jnp.arange(RET_CHUNK, dtype=F32)
    rel = pos[:, None] - pos[None, :]
    intra = jnp.where(rel >= 0, jnp.exp(log_g[:, None, None] * jnp.maximum(rel, 0.0)), 0.0)
    q_decay = jnp.exp(log_g[:, None] * (pos + 1.0))
    k_decay = jnp.exp(log_g[:, None] * (RET_CHUNK - 1.0 - pos))
    chunk_decay = jnp.exp(log_g * RET_CHUNK)
    qd = jnp.broadcast_to(q_decay[:, :, None], (RET_HEADS, RET_CHUNK, RET_DK))
    kd = jnp.broadcast_to(k_decay[:, :, None], (RET_HEADS, RET_CHUNK, RET_DK))
    cd = jnp.broadcast_to(chunk_decay[:, None, None], (RET_HEADS, 1, RET_DV))
    return intra, qd, kd, cd


def _layer(x, positions, norm_mix_g, w_in, lam_re, lam_im, log_dt, b_re, b_im, c_re, c_im,
           s5_d, w_glu, b_glu, ret_gn_g, w_branch_s5, w_branch_ret, w_out, norm_ffn_g,
           router_w, router_b, w1, b1, w2, b2, g_out):
    bsz, seqlen, _ = x.shape
    n_tok = bsz * seqlen

    half = RET_DK // 2
    inv_freq = ROPE_BASE ** (-jnp.arange(half, dtype=F32) / half)
    rope = jnp.stack([jnp.concatenate([inv_freq, inv_freq]),
                      jnp.concatenate([-jnp.ones((half,), F32), jnp.ones((half,), F32)])])
    pos128 = jnp.broadcast_to(positions.astype(F32)[..., None], (bsz, seqlen, LANES))
    u, q, k, v, g_ret, gate_a, gate_b = _inproj(
        x, pos128, rope, norm_mix_g.reshape(1, D_MODEL), w_in.astype(BF16))

    lam = jnp.stack([lam_re.reshape(-1), lam_im.reshape(-1),
                     jnp.repeat(log_dt, S5_STATE)]).astype(F32)
    bblk = jnp.stack([_block_diag(jnp.swapaxes(b_re, 1, 2)), _block_diag(jnp.swapaxes(b_im, 1, 2))])
    cblk = jnp.stack([_block_diag(jnp.swapaxes(c_re, 1, 2)),
                      _block_diag(jnp.swapaxes(c_im, 1, 2))]).astype(BF16)
    dvec = jnp.stack([s5_d, b_glu])
    ys5 = _s5(u.reshape(seqlen * bsz, S5_WIDTH), lam, bblk, cblk, dvec, w_glu.astype(BF16), bsz)
    ys5 = ys5.reshape(seqlen, bsz * S5_WIDTH)

    intra, qd, kd, cd = _retention_tables()
    retg = _retention(q, k, v, g_ret, intra, qd, kd, cd, ret_gn_g.reshape(1, RET_V_WIDTH))

    rw = jnp.zeros((D_MODEL, LANES), F32).at[:, :N_EXPERTS].set(router_w)
    rb = jnp.zeros((1, LANES), F32).at[0, :N_EXPERTS].set(router_b)
    x1, h2, route, gates, counts = _merge(
        x, ys5, retg, gate_a, gate_b, w_branch_s5.astype(BF16), w_branch_ret.astype(BF16),
        w_out.astype(BF16), norm_ffn_g.reshape(1, D_MODEL), rw, rb)

    bm = BM_EXPERT
    n_assign = n_tok * TOP_K
    n_rows = (n_assign // bm + N_EXPERTS) * bm
    route = route.reshape(n_tok, LANES)
    top_e = route[:, :TOP_K]
    rank = route[:, TOP_K:2 * TOP_K]
    cnt = counts[0, :N_EXPERTS].astype(jnp.int32)
    padded = (cnt + bm - 1) // bm * bm
    pend = jnp.cumsum(padded)
    pstart = pend - padded
    dest = (pstart[top_e] + rank).astype(jnp.int32)
    block_start = jnp.arange(n_rows // bm, dtype=jnp.int32) * bm
    block_expert = jnp.minimum(jnp.searchsorted(pend, block_start, side='right'),
                               N_EXPERTS - 1).astype(jnp.int32)

    x_rows = _dispatch(dest, h2.reshape(n_tok, D_MODEL), jnp.zeros((n_rows, D_MODEL), F32))
    w1p = jnp.concatenate([w1[:, :, 0::2], w1[:, :, 1::2]], axis=-1).astype(BF16)
    b1p = jnp.concatenate([b1[:, 0::2], b1[:, 1::2]], axis=-1).reshape(N_EXPERTS, 1, 2 * D_FF)
    y_rows = _experts(block_expert, x_rows, w1p, b1p, w2.astype(BF16),
                      b2.reshape(N_EXPERTS, 1, D_MODEL))
    out = _combine(dest, gates.reshape(n_tok, LANES), x1.reshape(n_tok, D_MODEL),
                   g_out.reshape(1, D_MODEL), y_rows)
    return out.reshape(bsz, seqlen, D_MODEL)


def kernel(x, positions, norm_mix_g, w_in, s5_lambda_re, s5_lambda_im, s5_log_dt, s5_b_re, s5_b_im, s5_c_re, s5_c_im, s5_d, s5_w_glu, s5_b_glu, ret_gn_g, w_branch_s5, w_branch_ret, w_out, norm_ffn_g, router_w, router_b, expert_w1, expert_b1, expert_w2, expert_b2, norm_final_g):
    assert norm_mix_g.shape[0] == 1, "single-layer trunk"
    return _layer(x, positions, norm_mix_g[0], w_in[0], s5_lambda_re[0], s5_lambda_im[0],
                  s5_log_dt[0], s5_b_re[0], s5_b_im[0], s5_c_re[0], s5_c_im[0], s5_d[0],
                  s5_w_glu[0], s5_b_glu[0], ret_gn_g[0], w_branch_s5[0], w_branch_ret[0],
                  w_out[0], norm_ffn_g[0], router_w[0], router_b[0], expert_w1[0], expert_b1[0],
                  expert_w2[0], expert_b2[0], norm_final_g)
```

```python
import functools
import math

import jax
import jax.numpy as jnp
from jax import lax
from jax.experimental import pallas as pl
from jax.experimental.pallas import tpu as pltpu

F32 = jnp.float32
BF16 = jnp.bfloat16

D_MODEL = 1024
S5_WIDTH = 512
S5_GROUP = 16
S5_GROUPS = 32
S5_STATE = 64
S5_LANES = S5_GROUPS * S5_STATE
RET_HEADS = 4
RET_DK = 128
RET_DV = 256
RET_QK_WIDTH = RET_HEADS * RET_DK
RET_V_WIDTH = RET_HEADS * RET_DV
RET_CHUNK = 128
ROPE_BASE = 10000.0
N_EXPERTS = 32
TOP_K = 4
D_FF = 1024
SWIGLU_ALPHA = 1.702
SWIGLU_LIMIT = 7.0
NORM_EPS = 1e-5
IN_WIDTH = S5_WIDTH + 2 * RET_QK_WIDTH + 2 * RET_V_WIDTH + 2 * D_MODEL
OFF_Q = S5_WIDTH
OFF_V = OFF_Q + 2 * RET_QK_WIDTH
OFF_G = OFF_V + RET_V_WIDTH
OFF_GA = OFF_G + RET_V_WIDTH
OFF_GB = OFF_GA + D_MODEL

LANES = 128
SUBLANES = 8
VMEM_LIMIT = 56 * 1024 * 1024

TM_PROJ = 512
TT_S5 = 128
S5_LC = 512
TR_RET = 512
BM_EXPERT = 512
TD_ROWS = 256


def _rms(x, g):
    return x * lax.rsqrt(jnp.mean(x * x, axis=-1, keepdims=True) + NORM_EPS) * g


def _inproj_body(x_ref, pos_ref, rope_ref, g_ref, w_ref,
                 u_ref, q_ref, k_ref, v_ref, gr_ref, ga_ref, gb_ref):
    hb = _rms(x_ref[0], g_ref[...]).astype(BF16)

    def proj(lo, width):
        return jnp.dot(hb, w_ref[:, lo:lo + width], preferred_element_type=F32)

    u_ref[...] = proj(0, S5_WIDTH).astype(BF16)
    ang = pos_ref[0] * rope_ref[0:1, :]
    cos = jnp.cos(ang)
    sin = jnp.sin(ang) * rope_ref[1:2, :]
    qk = proj(OFF_Q, 2 * RET_QK_WIDTH)
    for h in range(RET_HEADS):
        qh = qk[:, h * RET_DK:(h + 1) * RET_DK]
        q_ref[0, :, h * RET_DK:(h + 1) * RET_DK] = (
            qh * cos + pltpu.roll(qh, RET_DK // 2, 1) * sin).astype(BF16)
        kh = qk[:, RET_QK_WIDTH + h * RET_DK:RET_QK_WIDTH + (h + 1) * RET_DK]
        k_ref[0, :, h * RET_DK:(h + 1) * RET_DK] = (
            (kh * cos + pltpu.roll(kh, RET_DK // 2, 1) * sin) * (RET_DK ** -0.5)).astype(BF16)
    v_ref[0] = proj(OFF_V, RET_V_WIDTH).astype(BF16)
    gr_ref[0] = proj(OFF_G, RET_V_WIDTH).astype(BF16)
    ga_ref[0] = proj(OFF_GA, D_MODEL).astype(BF16)
    gb_ref[0] = proj(OFF_GB, D_MODEL).astype(BF16)


def _inproj(x, pos128, rope, g, w_bf):
    bsz, seqlen, _ = x.shape
    tm = min(TM_PROJ, seqlen)
    grid = (bsz, seqlen // tm)
    row = lambda b, t: (b, t, 0)
    const = lambda b, t: (0, 0)
    out_shape = (
        jax.ShapeDtypeStruct((seqlen, bsz * S5_WIDTH), BF16),
        jax.ShapeDtypeStruct((bsz, seqlen, RET_QK_WIDTH), BF16),
        jax.ShapeDtypeStruct((bsz, seqlen, RET_QK_WIDTH), BF16),
        jax.ShapeDtypeStruct((bsz, seqlen, RET_V_WIDTH), BF16),
        jax.ShapeDtypeStruct((bsz, seqlen, RET_V_WIDTH), BF16),
        jax.ShapeDtypeStruct((bsz, seqlen, D_MODEL), BF16),
        jax.ShapeDtypeStruct((bsz, seqlen, D_MODEL), BF16),
    )
    return pl.pallas_call(
        _inproj_body,
        grid=grid,
        in_specs=[
            pl.BlockSpec((1, tm, D_MODEL), row),
            pl.BlockSpec((1, tm, LANES), row),
            pl.BlockSpec((2, LANES), const),
            pl.BlockSpec((1, D_MODEL), const),
            pl.BlockSpec((D_MODEL, IN_WIDTH), const, pipeline_mode=pl.Buffered(1)),
        ],
        out_specs=(
            pl.BlockSpec((tm, S5_WIDTH), lambda b, t: (t, b)),
            pl.BlockSpec((1, tm, RET_QK_WIDTH), row),
            pl.BlockSpec((1, tm, RET_QK_WIDTH), row),
            pl.BlockSpec((1, tm, RET_V_WIDTH), row),
            pl.BlockSpec((1, tm, RET_V_WIDTH), row),
            pl.BlockSpec((1, tm, D_MODEL), row),
            pl.BlockSpec((1, tm, D_MODEL), row),
        ),
        out_shape=out_shape,
        compiler_params=pltpu.CompilerParams(
            dimension_semantics=("arbitrary", "arbitrary"), vmem_limit_bytes=VMEM_LIMIT),
        name="inproj",
    )(x, pos128, rope, g, w_bf)


def _s5_body(u_ref, lam_ref, bblk_ref, cblk_ref, dvec_ref, wglu_ref, o_ref,
             bf_ref, a8_ref, p_ref, xre_ref, xim_ref):
    rows = u_ref.shape[0]

    @pl.when(pl.program_id(0) == 0)
    def _init():
        lr = lam_ref[0:1, :]
        li = lam_ref[1:2, :]
        dt = jnp.exp(lam_ref[2:3, :])
        mag = jnp.exp(lr * dt)
        a_re = mag * jnp.cos(li * dt)
        a_im = mag * jnp.sin(li * dt)
        nr = a_re - 1.0
        den = lr * lr + li * li
        f_re = (nr * lr + a_im * li) / den
        f_im = (a_im * lr - nr * li) / den
        b_re = bblk_ref[0]
        b_im = bblk_ref[1]
        bf_ref[0] = (b_re * f_re - b_im * f_im).astype(BF16)
        bf_ref[1] = (b_im * f_re + b_re * f_im).astype(BF16)
        a8_ref[0] = jnp.broadcast_to(a_re, (SUBLANES, S5_LANES))
        a8_ref[1] = jnp.broadcast_to(a_im, (SUBLANES, S5_LANES))
        p_ref[...] = jnp.zeros_like(p_ref)

    ub = u_ref[...]
    xre_ref[...] = jnp.dot(ub, bf_ref[0], preferred_element_type=F32)
    xim_ref[...] = jnp.dot(ub, bf_ref[1], preferred_element_type=F32)

    lower = lax.broadcasted_iota(jnp.int32, (SUBLANES, S5_LC), 0) < (SUBLANES // 2)
    for j in range(S5_LANES // S5_LC):
        sl = slice(j * S5_LC, (j + 1) * S5_LC)
        ar = a8_ref[0, :, sl]
        ai = a8_ref[1, :, sl]

        def step(k, carry, sl=sl, ar=ar, ai=ai):
            pr, pi = carry
            r0 = pl.multiple_of(k * SUBLANES, SUBLANES)
            xr = xre_ref[pl.ds(r0, SUBLANES), sl]
            xi = xim_ref[pl.ds(r0, SUBLANES), sl]
            s1r = ar * pr - ai * pi + xr
            s1i = ar * pi + ai * pr + xi
            tr = pltpu.roll(s1r, SUBLANES // 2, 0)
            ti = pltpu.roll(s1i, SUBLANES // 2, 0)
            s2r = ar * tr - ai * ti + xr
            s2i = ar * ti + ai * tr + xi
            xre_ref[pl.ds(r0, SUBLANES), sl] = jnp.where(lower, s1r, s2r)
            xim_ref[pl.ds(r0, SUBLANES), sl] = jnp.where(lower, s1i, s2i)
            return pltpu.roll(s2r, SUBLANES // 2, 0), pltpu.roll(s2i, SUBLANES // 2, 0)

        pr, pi = lax.fori_loop(0, rows // SUBLANES, step, (p_ref[0, :, sl], p_ref[1, :, sl]),
                               unroll=2)
        p_ref[0, :, sl] = pr
        p_ref[1, :, sl] = pi

    y = (jnp.dot(xre_ref[...].astype(BF16), cblk_ref[0], preferred_element_type=F32)
         - jnp.dot(xim_ref[...].astype(BF16), cblk_ref[1], preferred_element_type=F32))
    y = y + dvec_ref[0:1, :] * ub.astype(F32)
    z = jax.nn.gelu(y)
    gate = jax.nn.sigmoid(
        jnp.dot(z.astype(BF16), wglu_ref[...], preferred_element_type=F32) + dvec_ref[1:2, :])
    o_ref[...] = (z * gate).astype(BF16)


def _s5(u2, lam, bblk, cblk, dvec, wglu, bsz):
    n_rows = u2.shape[0]
    rows = min(TT_S5 * bsz, n_rows)
    const2 = lambda i: (0, 0)
    const3 = lambda i: (0, 0, 0)
    return pl.pallas_call(
        _s5_body,
        grid=(n_rows // rows,),
        in_specs=[
            pl.BlockSpec((rows, S5_WIDTH), lambda i: (i, 0)),
            pl.BlockSpec((3, S5_LANES), const2),
            pl.BlockSpec((2, S5_WIDTH, S5_LANES), const3),
            pl.BlockSpec((2, S5_LANES, S5_WIDTH), const3),
            pl.BlockSpec((2, S5_WIDTH), const2),
            pl.BlockSpec((S5_WIDTH, S5_WIDTH), const2),
        ],
        out_specs=pl.BlockSpec((rows, S5_WIDTH), lambda i: (i, 0)),
        out_shape=jax.ShapeDtypeStruct((n_rows, S5_WIDTH), BF16),
        scratch_shapes=[
            pltpu.VMEM((2, S5_WIDTH, S5_LANES), BF16),
            pltpu.VMEM((2, SUBLANES, S5_LANES), F32),
            pltpu.VMEM((2, SUBLANES, S5_LANES), F32),
            pltpu.VMEM((rows, S5_LANES), F32),
            pltpu.VMEM((rows, S5_LANES), F32),
        ],
        compiler_params=pltpu.CompilerParams(
            dimension_semantics=("arbitrary",), vmem_limit_bytes=VMEM_LIMIT),
        name="s5_mixer",
    )(u2, lam, bblk, cblk, dvec, wglu)


def _ret_body(q_ref, k_ref, v_ref, g_ref, intra_ref, qd_ref, kd_ref, cd_ref, gn_ref, o_ref,
              st_ref):
    tr = q_ref.shape[1]

    @pl.when(pl.program_id(1) == 0)
    def _init():
        st_ref[...] = jnp.zeros_like(st_ref)

    for h in range(RET_HEADS):
        qs = slice(h * RET_DK, (h + 1) * RET_DK)
        vs = slice(h * RET_DV, (h + 1) * RET_DV)
        for c in range(tr // RET_CHUNK):
            rs = slice(c * RET_CHUNK, (c + 1) * RET_CHUNK)
            qc = q_ref[0, rs, qs]
            kc = k_ref[0, rs, qs]
            vc = v_ref[0, rs, vs]
            scores = lax.dot_general(qc, kc, (((1,), (1,)), ((), ())),
                                     preferred_element_type=F32) * intra_ref[h]
            inner = jnp.dot(scores.astype(BF16), vc, preferred_element_type=F32)
            st = st_ref[h]
            qdec = (qc.astype(F32) * qd_ref[h]).astype(BF16)
            cross = jnp.dot(qdec, st.astype(BF16), preferred_element_type=F32)
            kdec_t = (kc.astype(F32) * kd_ref[h]).T.astype(BF16)
            st_ref[h] = cd_ref[h] * st + jnp.dot(kdec_t, vc, preferred_element_type=F32)
            ret = inner + cross
            mu = jnp.mean(ret, axis=-1, keepdims=True)
            dev = ret - mu
            var = jnp.mean(dev * dev, axis=-1, keepdims=True)
            yn = dev * lax.rsqrt(var + NORM_EPS) * gn_ref[0:1, vs]
            o_ref[0, rs, vs] = (jax.nn.silu(g_ref[0, rs, vs].astype(F32)) * yn).astype(BF16)


def _retention(q, k, v, g, intra, qd, kd, cd, gn):
    bsz, seqlen, _ = q.shape
    tr = min(TR_RET, seqlen)
    row = lambda b, t: (b, t, 0)
    const3 = lambda b, t: (0, 0, 0)
    return pl.pallas_call(
        _ret_body,
        grid=(bsz, seqlen // tr),
        in_specs=[
            pl.BlockSpec((1, tr, RET_QK_WIDTH), row),
            pl.BlockSpec((1, tr, RET_QK_WIDTH), row),
            pl.BlockSpec((1, tr, RET_V_WIDTH), row),
            pl.BlockSpec((1, tr, RET_V_WIDTH), row),
            pl.BlockSpec((RET_HEADS, RET_CHUNK, RET_CHUNK), const3),
            pl.BlockSpec((RET_HEADS, RET_CHUNK, RET_DK), const3),
            pl.BlockSpec((RET_HEADS, RET_CHUNK, RET_DK), const3),
            pl.BlockSpec((RET_HEADS, 1, RET_DV), const3),
            pl.BlockSpec((1, RET_V_WIDTH), lambda b, t: (0, 0)),
        ],
        out_specs=pl.BlockSpec((1, tr, RET_V_WIDTH), row),
        out_shape=jax.ShapeDtypeStruct((bsz, seqlen, RET_V_WIDTH), BF16),
        scratch_shapes=[pltpu.VMEM((RET_HEADS, RET_DK, RET_DV), F32)],
        compiler_params=pltpu.CompilerParams(
            dimension_semantics=("arbitrary", "arbitrary"), vmem_limit_bytes=VMEM_LIMIT),
        name="retention",
    )(q, k, v, g, intra, qd, kd, cd, gn)


def _merge_body(x_ref, ys_ref, rg_ref, ga_ref, gb_ref, wbs_ref, wbr_ref, wo_ref, gf_ref,
                rw_ref, rb_ref,
                x1_ref, h2_ref, ri_ref, gt_ref, cnt_ref, run_ref):
    tm = x_ref.shape[1]

    @pl.when((pl.program_id(0) == 0) & (pl.program_id(1) == 0))
    def _init():
        run_ref[...] = jnp.zeros_like(run_ref)

    y_a = jnp.dot(ys_ref[...], wbs_ref[...], preferred_element_type=F32)
    y_b = jnp.dot(rg_ref[0], wbr_ref[...], preferred_element_type=F32)
    merged = (jax.nn.sigmoid(ga_ref[0].astype(F32)) * y_a
              + jax.nn.sigmoid(gb_ref[0].astype(F32)) * y_b)
    x1 = x_ref[0] + jnp.dot(merged.astype(BF16), wo_ref[...], preferred_element_type=F32)
    x1_ref[0] = x1
    h2 = _rms(x1, gf_ref[...])
    h2_ref[0] = h2

    lane = lax.broadcasted_iota(jnp.int32, (tm, LANES), 1)
    logits = jnp.dot(h2, rw_ref[...], preferred_element_type=F32,
                     precision=lax.Precision.HIGHEST) + rb_ref[...]
    work = jnp.where(lane < N_EXPERTS, logits, -jnp.inf)
    lane_f = lane.astype(F32)
    vals, hots = [], []
    ri = jnp.zeros((tm, LANES), jnp.int32)
    for kk in range(TOP_K):
        m = jnp.max(work, axis=-1, keepdims=True)
        idx = jnp.min(jnp.where(work == m, lane_f, float(LANES)), axis=-1, keepdims=True)
        hot = lane_f == idx
        vals.append(m)
        hots.append(hot)
        ri = jnp.where(lane == kk, idx.astype(jnp.int32), ri)
        work = jnp.where(hot, -jnp.inf, work)
    exps = [jnp.exp(v - vals[0]) for v in vals]
    denom = exps[0] + exps[1] + exps[2] + exps[3]
    gt = jnp.zeros((tm, LANES), F32)
    for kk in range(TOP_K):
        gt = jnp.where(lane == kk, exps[kk] / denom, gt)
    gt_ref[0] = gt

    onehot = [h.astype(F32) for h in hots]
    tot = onehot[0] + onehot[1] + onehot[2] + onehot[3]
    r_i = lax.broadcasted_iota(jnp.int32, (tm, tm), 0)
    c_i = lax.broadcasted_iota(jnp.int32, (tm, tm), 1)
    tril = (c_i < r_i).astype(BF16)
    base = jnp.dot(tril, tot.astype(BF16), preferred_element_type=F32) + run_ref[...]
    for kk in range(TOP_K):
        rank = jnp.sum(onehot[kk] * base, axis=-1, keepdims=True).astype(jnp.int32)
        ri = jnp.where(lane == TOP_K + kk, rank, ri)
    ri_ref[0] = ri
    run_ref[...] = run_ref[...] + jnp.sum(tot, axis=0, keepdims=True)
    cnt_ref[...] = run_ref[...]


def _merge(x, ys5, retg, ga, gb, wbs, wbr, wo, gf, rw, rb):
    bsz, seqlen, _ = x.shape
    tm = min(TM_PROJ, seqlen)
    row = lambda b, t: (b, t, 0)
    const = lambda b, t: (0, 0)
    out_shape = (
        jax.ShapeDtypeStruct((bsz, seqlen, D_MODEL), F32),
        jax.ShapeDtypeStruct((bsz, seqlen, D_MODEL), F32),
        jax.ShapeDtypeStruct((bsz, seqlen, LANES), jnp.int32),
        jax.ShapeDtypeStruct((bsz, seqlen, LANES), F32),
        jax.ShapeDtypeStruct((1, LANES), F32),
    )
    return pl.pallas_call(
        _merge_body,
        grid=(bsz, seqlen // tm),
        in_specs=[
            pl.BlockSpec((1, tm, D_MODEL), row),
            pl.BlockSpec((tm, S5_WIDTH), lambda b, t: (t, b)),
            pl.BlockSpec((1, tm, RET_V_WIDTH), row),
            pl.BlockSpec((1, tm, D_MODEL), row),
            pl.BlockSpec((1, tm, D_MODEL), row),
            pl.BlockSpec((S5_WIDTH, D_MODEL), const),
            pl.BlockSpec((RET_V_WIDTH, D_MODEL), const),
            pl.BlockSpec((D_MODEL, D_MODEL), const),
            pl.BlockSpec((1, D_MODEL), const),
            pl.BlockSpec((D_MODEL, LANES), const),
            pl.BlockSpec((1, LANES), const),
        ],
        out_specs=(
            pl.BlockSpec((1, tm, D_MODEL), row),
            pl.BlockSpec((1, tm, D_MODEL), row),
            pl.BlockSpec((1, tm, LANES), row),
            pl.BlockSpec((1, tm, LANES), row),
            pl.BlockSpec((1, LANES), const),
        ),
        out_shape=out_shape,
        scratch_shapes=[pltpu.VMEM((1, LANES), F32)],
        compiler_params=pltpu.CompilerParams(
            dimension_semantics=("arbitrary", "arbitrary"), vmem_limit_bytes=VMEM_LIMIT),
        name="merge_router",
    )(x, ys5, retg, ga, gb, wbs, wbr, wo, gf, rw, rb)


def _dispatch_body(z_ref, dest_ref, h_ref, xout_ref, zbuf_ref, sem, zsem):
    td = h_ref.shape[0]
    bm = zbuf_ref.shape[0]
    n_blocks = xout_ref.shape[0] // bm

    @pl.when(pl.program_id(0) == 0)
    def _zero_fill():
        zbuf_ref[...] = jnp.zeros_like(zbuf_ref)

        def pad_row(e, i):
            return pltpu.make_async_copy(
                zbuf_ref.at[pl.ds(0, 1)], xout_ref.at[pl.ds(z_ref[e] + i, 1)], zsem)

        def tail_block(j):
            return pltpu.make_async_copy(
                zbuf_ref, xout_ref.at[pl.ds(pl.multiple_of(j * bm, bm), bm)], zsem)

        def over_fill_copies(fn):
            def per_expert(e, carry):
                def per_row(i, c):
                    fn(pad_row(e, i))
                    return c
                return lax.fori_loop(0, z_ref[N_EXPERTS + e], per_row, carry)

            def per_block(j, c):
                fn(tail_block(j))
                return c

            lax.fori_loop(0, N_EXPERTS, per_expert, 0)
            lax.fori_loop(z_ref[2 * N_EXPERTS], n_blocks, per_block, 0)

        over_fill_copies(lambda cp: cp.start())
        over_fill_copies(lambda cp: cp.wait())

    def copy(r, kk):
        d = dest_ref[0, 0, r * TOP_K + kk]
        return pltpu.make_async_copy(h_ref.at[pl.ds(r, 1)], xout_ref.at[pl.ds(d, 1)], sem)

    def start(r, c):
        for kk in range(TOP_K):
            copy(r, kk).start(priority=kk % 2)
        return c

    def wait(r, c):
        for kk in range(TOP_K):
            copy(r, kk).wait()
        return c

    lax.fori_loop(0, td, start, 0, unroll=8)
    lax.fori_loop(0, td, wait, 0, unroll=8)


def _dispatch(zinfo, dest, h2, n_rows, bm):
    n_tok = h2.shape[0]
    td = min(TD_ROWS, n_tok)
    dest3 = dest.reshape(n_tok // td, 1, td * TOP_K)
    grid_spec = pltpu.PrefetchScalarGridSpec(
        num_scalar_prefetch=1,
        grid=(n_tok // td,),
        in_specs=[
            pl.BlockSpec((1, 1, td * TOP_K), lambda i, z: (i, 0, 0), memory_space=pltpu.SMEM),
            pl.BlockSpec((td, D_MODEL), lambda i, z: (i, 0)),
        ],
        out_specs=pl.BlockSpec(memory_space=pl.ANY),
        scratch_shapes=[pltpu.VMEM((bm, D_MODEL), F32), pltpu.SemaphoreType.DMA(()),
                        pltpu.SemaphoreType.DMA(())],
    )
    return pl.pallas_call(
        _dispatch_body,
        grid_spec=grid_spec,
        out_shape=jax.ShapeDtypeStruct((n_rows, D_MODEL), F32),
        compiler_params=pltpu.CompilerParams(
            dimension_semantics=("arbitrary",), has_side_effects=True),
        name="dispatch",
    )(zinfo, dest3, h2)


def _w1prep_body(w_ref, p_ref, o_ref):
    for c in range(D_FF // LANES):
        wc = w_ref[0, :, c * 2 * LANES:(c + 1) * 2 * LANES].astype(BF16)
        r = jnp.dot(wc, p_ref[...], preferred_element_type=F32).astype(BF16)
        o_ref[0, :, c * LANES:(c + 1) * LANES] = r[:, :LANES]
        o_ref[0, :, D_FF + c * LANES:D_FF + (c + 1) * LANES] = r[:, LANES:]


def _w1prep(w1):
    i = lax.broadcasted_iota(jnp.int32, (2 * LANES, 2 * LANES), 0)
    j = lax.broadcasted_iota(jnp.int32, (2 * LANES, 2 * LANES), 1)
    perm = (i == jnp.where(j < LANES, 2 * j, 2 * (j - LANES) + 1)).astype(BF16)
    return pl.pallas_call(
        _w1prep_body,
        grid=(N_EXPERTS,),
        in_specs=[
            pl.BlockSpec((1, D_MODEL, 2 * D_FF), lambda e: (e, 0, 0)),
            pl.BlockSpec((2 * LANES, 2 * LANES), lambda e: (0, 0)),
        ],
        out_specs=pl.BlockSpec((1, D_MODEL, 2 * D_FF), lambda e: (e, 0, 0)),
        out_shape=jax.ShapeDtypeStruct((N_EXPERTS, D_MODEL, 2 * D_FF), BF16),
        compiler_params=pltpu.CompilerParams(
            dimension_semantics=("arbitrary",), vmem_limit_bytes=VMEM_LIMIT),
        name="w1_prep",
    )(w1, perm)


def _expert_body(be_ref, na_ref, x_ref, w1_ref, b1_ref, w2_ref, b2_ref, y_ref):
    del be_ref
    active = pl.program_id(0) < na_ref[0]

    @pl.when(active)
    def _compute():
        xb = x_ref[...].astype(BF16)
        gu = jnp.dot(xb, w1_ref[0], preferred_element_type=F32) + b1_ref[0]
        x_glu = jnp.minimum(gu[:, :D_FF], SWIGLU_LIMIT)
        x_lin = jnp.clip(gu[:, D_FF:], -SWIGLU_LIMIT, SWIGLU_LIMIT)
        act = x_glu * jax.nn.sigmoid(SWIGLU_ALPHA * x_glu) * (x_lin + 1.0)
        y_ref[...] = jnp.dot(act.astype(BF16), w2_ref[0], preferred_element_type=F32) + b2_ref[0]

    @pl.when(jnp.logical_not(active))
    def _unused_block():
        y_ref[...] = jnp.zeros_like(y_ref)


def _experts(block_expert, n_active, x_rows, w1, b1, w2, b2, bm):
    n_rows = x_rows.shape[0]
    grid_spec = pltpu.PrefetchScalarGridSpec(
        num_scalar_prefetch=2,
        grid=(n_rows // bm,),
        in_specs=[
            pl.BlockSpec((bm, D_MODEL), lambda i, be, na: (i, 0)),
            pl.BlockSpec((1, D_MODEL, 2 * D_FF), lambda i, be, na: (be[i], 0, 0)),
            pl.BlockSpec((1, 1, 2 * D_FF), lambda i, be, na: (be[i], 0, 0)),
            pl.BlockSpec((1, D_FF, D_MODEL), lambda i, be, na: (be[i], 0, 0)),
            pl.BlockSpec((1, 1, D_MODEL), lambda i, be, na: (be[i], 0, 0)),
        ],
        out_specs=pl.BlockSpec((bm, D_MODEL), lambda i, be, na: (i, 0)),
    )
    return pl.pallas_call(
        _expert_body,
        grid_spec=grid_spec,
        out_shape=jax.ShapeDtypeStruct((n_rows, D_MODEL), F32),
        compiler_params=pltpu.CompilerParams(
            dimension_semantics=("arbitrary",), vmem_limit_bytes=VMEM_LIMIT),
        name="expert_ffn",
    )(block_expert, n_active, x_rows, w1, b1, w2, b2)


def _combine_body(dest_ref, gt_ref, x1_ref, gfin_ref, yrows_ref, o_ref, ybuf_ref, sem):
    td = x1_ref.shape[0]

    def copy(r, kk):
        d = dest_ref[0, 0, r * TOP_K + kk]
        return pltpu.make_async_copy(yrows_ref.at[pl.ds(d, 1)], ybuf_ref.at[kk, pl.ds(r, 1)], sem)

    def start(r, c):
        for kk in range(TOP_K):
            copy(r, kk).start(priority=kk % 2)
        return c

    def wait(r, c):
        for kk in range(TOP_K):
            copy(r, kk).wait()
        return c

    lax.fori_loop(0, td, start, 0, unroll=8)
    lax.fori_loop(0, td, wait, 0, unroll=8)
    gt = gt_ref[...]
    acc = x1_ref[...]
    for kk in range(TOP_K):
        acc = acc + gt[:, kk:kk + 1] * ybuf_ref[kk]
    o_ref[...] = _rms(acc, gfin_ref[...])


def _combine(dest, gates, x1, gfin, y_rows):
    n_tok = x1.shape[0]
    td = min(TD_ROWS, n_tok)
    dest3 = dest.reshape(n_tok // td, 1, td * TOP_K)
    return pl.pallas_call(
        _combine_body,
        grid=(n_tok // td,),
        in_specs=[
            pl.BlockSpec((1, 1, td * TOP_K), lambda i: (i, 0, 0), memory_space=pltpu.SMEM),
            pl.BlockSpec((td, LANES), lambda i: (i, 0)),
            pl.BlockSpec((td, D_MODEL), lambda i: (i, 0)),
            pl.BlockSpec((1, D_MODEL), lambda i: (0, 0)),
            pl.BlockSpec(memory_space=pl.ANY),
        ],
        out_specs=pl.BlockSpec((td, D_MODEL), lambda i: (i, 0)),
        out_shape=jax.ShapeDtypeStruct((n_tok, D_MODEL), F32),
        scratch_shapes=[pltpu.VMEM((TOP_K, td, D_MODEL), F32), pltpu.SemaphoreType.DMA(())],
        compiler_params=pltpu.CompilerParams(
            dimension_semantics=("arbitrary",), vmem_limit_bytes=VMEM_LIMIT),
        name="combine",
    )(dest3, gates, x1, gfin, y_rows)


def _block_diag(p):
    g, a, b = p.shape
    eye = jnp.eye(g, dtype=p.dtype)
    return (p[:, :, None, :] * eye[:, None, :, None]).reshape(g * a, g * b)


def _retention_tables():
    log_g = jnp.log1p(-(2.0 ** (-5.0 - jnp.arange(RET_HEADS, dtype=F32))))
    pos = jnp.arange(RET_CHUNK, dtype=F32)
    rel = pos[:, None] - pos[None, :]
    intra = jnp.where(rel >= 0, jnp.exp(log_g[:, None, None] * jnp.maximum(rel, 0.0)), 0.0)
    q_decay = jnp.exp(log_g[:, None] * (pos + 1.0))
    k_decay = jnp.exp(log_g[:, None] * (RET_CHUNK - 1.0 - pos))
    chunk_decay = jnp.exp(log_g * RET_CHUNK)
    qd = jnp.broadcast_to(q_decay[:, :, None], (RET_HEADS, RET_CHUNK, RET_DK))
    kd = jnp.broadcast_to(k_decay[:, :, None], (RET_HEADS, RET_CHUNK, RET_DK))
    cd = jnp.broadcast_to(chunk_decay[:, None, None], (RET_HEADS, 1, RET_DV))
    return intra, qd, kd, cd


def _layer(x, positions, norm_mix_g, w_in, lam_re, lam_im, log_dt, b_re, b_im, c_re, c_im,
           s5_d, w_glu, b_glu, ret_gn_g, w_branch_s5, w_branch_ret, w_out, norm_ffn_g,
           router_w, router_b, w1, b1, w2, b2, g_out):
    bsz, seqlen, _ = x.shape
    n_tok = bsz * seqlen

    half = RET_DK // 2
    inv_freq = ROPE_BASE ** (-jnp.arange(half, dtype=F32) / half)
    rope = jnp.stack([jnp.concatenate([inv_freq, inv_freq]),
                      jnp.concatenate([-jnp.ones((half,), F32), jnp.ones((half,), F32)])])
    pos128 = jnp.broadcast_to(positions.astype(F32)[..., None], (bsz, seqlen, LANES))
    u, q, k, v, g_ret, gate_a, gate_b = _inproj(
        x, pos128, rope, norm_mix_g.reshape(1, D_MODEL), w_in.astype(BF16))

    lam = jnp.stack([lam_re.reshape(-1), lam_im.reshape(-1),
                     jnp.repeat(log_dt, S5_STATE)]).astype(F32)
    bblk = jnp.stack([_block_diag(jnp.swapaxes(b_re, 1, 2)), _block_diag(jnp.swapaxes(b_im, 1, 2))])
    cblk = jnp.stack([_block_diag(jnp.swapaxes(c_re, 1, 2)),
                      _block_diag(jnp.swapaxes(c_im, 1, 2))]).astype(BF16)
    dvec = jnp.stack([s5_d, b_glu])
    ys5 = _s5(u.reshape(seqlen * bsz, S5_WIDTH), lam, bblk, cblk, dvec, w_glu.astype(BF16), bsz)
    ys5 = ys5.reshape(seqlen, bsz * S5_WIDTH)

    intra, qd, kd, cd = _retention_tables()
    retg = _retention(q, k, v, g_ret, intra, qd, kd, cd, ret_gn_g.reshape(1, RET_V_WIDTH))

    rw = jnp.zeros((D_MODEL, LANES), F32).at[:, :N_EXPERTS].set(router_w)
    rb = jnp.zeros((1, LANES), F32).at[0, :N_EXPERTS].set(router_b)
    x1, h2, route, gates, counts = _merge(
        x, ys5, retg, gate_a, gate_b, w_branch_s5.astype(BF16), w_branch_ret.astype(BF16),
        w_out.astype(BF16), norm_ffn_g.reshape(1, D_MODEL), rw, rb)

    bm = BM_EXPERT
    n_assign = n_tok * TOP_K
    n_rows = (n_assign // bm + N_EXPERTS) * bm
    route = route.reshape(n_tok, LANES)
    top_e = route[:, :TOP_K]
    rank = route[:, TOP_K:2 * TOP_K]
    cnt = counts[0, :N_EXPERTS].astype(jnp.int32)
    padded = (cnt + bm - 1) // bm * bm
    pend = jnp.cumsum(padded)
    pstart = pend - padded
    dest = (pstart[top_e] + rank).astype(jnp.int32)
    block_start = jnp.arange(n_rows // bm, dtype=jnp.int32) * bm
    block_expert = jnp.minimum(
        jnp.sum((pend[None, :] <= block_start[:, None]).astype(jnp.int32), axis=1), N_EXPERTS - 1)
    n_active = (pend[N_EXPERTS - 1:] // bm).astype(jnp.int32)
    zinfo = jnp.concatenate([pstart + cnt, padded - cnt, n_active]).astype(jnp.int32)

    x_rows = _dispatch(zinfo, dest, h2.reshape(n_tok, D_MODEL), n_rows, bm)
    b1p = jnp.concatenate([b1[:, 0::2], b1[:, 1::2]], axis=-1).reshape(N_EXPERTS, 1, 2 * D_FF)
    y_rows = _experts(block_expert, n_active, x_rows, _w1prep(w1), b1p, w2.astype(BF16),
                      b2.reshape(N_EXPERTS, 1, D_MODEL), bm)
    out = _combine(dest, gates.reshape(n_tok, LANES), x1.reshape(n_tok, D_MODEL),
                   g_out.reshape(1, D_MODEL), y_rows)
    return out.reshape(bsz, seqlen, D_MODEL)


def kernel(x, positions, norm_mix_g, w_in, s5_lambda_re, s5_lambda_im, s5_log_dt, s5_b_re, s5_b_im, s5_c_re, s5_c_im, s5_d, s5_w_glu, s5_b_glu, ret_gn_g, w_branch_s5, w_branch_ret, w_out, norm_ffn_g, router_w, router_b, expert_w1, expert_b1, expert_w2, expert_b2, norm_final_g):
    assert norm_mix_g.shape[0] == 1, "single-layer trunk"
    return _layer(x, positions, norm_mix_g[0], w_in[0], s5_lambda_re[0], s5_lambda_im[0],
                  s5_log_dt[0], s5_b_re[0], s5_b_im[0], s5_c_re[0], s5_c_im[0], s5_d[0],
                  s5_w_glu[0], s5_b_glu[0], ret_gn_g[0], w_branch_s5[0], w_branch_ret[0],
                  w_out[0], norm_ffn_g[0], router_w[0], router_b[0], expert_w1[0], expert_b1[0],
                  expert_w2[0], expert_b2[0], norm_final_g)
```

```python
import jax
import jax.numpy as jnp
from jax import lax
from jax.experimental import pallas as pl
from jax.experimental.pallas import tpu as pltpu

F32 = jnp.float32
BF16 = jnp.bfloat16

D_MODEL = 1024
S5_WIDTH = 512
S5_GROUP = 16
S5_GROUPS = 32
S5_STATE = 64
S5_LANES = S5_GROUPS * S5_STATE
RET_HEADS = 4
RET_DK = 128
RET_DV = 256
RET_QK_WIDTH = RET_HEADS * RET_DK
RET_V_WIDTH = RET_HEADS * RET_DV
RET_CHUNK = 128
ROPE_BASE = 10000.0
N_EXPERTS = 32
TOP_K = 4
D_FF = 1024
SWIGLU_ALPHA = 1.702
SWIGLU_LIMIT = 7.0
NORM_EPS = 1e-5
IN_WIDTH = S5_WIDTH + 2 * RET_QK_WIDTH + 2 * RET_V_WIDTH + 2 * D_MODEL
OFF_Q = S5_WIDTH
OFF_V = OFF_Q + 2 * RET_QK_WIDTH
OFF_G = OFF_V + RET_V_WIDTH
OFF_GA = OFF_G + RET_V_WIDTH
OFF_GB = OFF_GA + D_MODEL

LANES = 128
SUBLANES = 8
VMEM_LIMIT = 56 * 1024 * 1024

TM_PROJ = 512
TT_S5 = 128
S5_LC = 512
S5_CHUNKS = 4
TR_RET = 512
BM_EXPERT = 512
TD_ROWS = 512


def _rms(x, g):
    return x * lax.rsqrt(jnp.mean(x * x, axis=-1, keepdims=True) + NORM_EPS) * g


def _inproj_body(x_ref, pos_ref, rope_ref, g_ref, w_ref,
                 u_ref, q_ref, k_ref, v_ref, gr_ref, ga_ref, gb_ref):
    hb = _rms(x_ref[0], g_ref[...]).astype(BF16)

    def proj(lo, width):
        return jnp.dot(hb, w_ref[:, lo:lo + width], preferred_element_type=F32)

    u_ref[...] = proj(0, S5_WIDTH).astype(BF16)
    ang = pos_ref[0] * rope_ref[0:1, :]
    cos = jnp.cos(ang)
    sin = jnp.sin(ang) * rope_ref[1:2, :]
    qk = proj(OFF_Q, 2 * RET_QK_WIDTH)
    for h in range(RET_HEADS):
        qh = qk[:, h * RET_DK:(h + 1) * RET_DK]
        q_ref[0, :, h * RET_DK:(h + 1) * RET_DK] = (
            qh * cos + pltpu.roll(qh, RET_DK // 2, 1) * sin).astype(BF16)
        kh = qk[:, RET_QK_WIDTH + h * RET_DK:RET_QK_WIDTH + (h + 1) * RET_DK]
        k_ref[0, :, h * RET_DK:(h + 1) * RET_DK] = (
            (kh * cos + pltpu.roll(kh, RET_DK // 2, 1) * sin) * (RET_DK ** -0.5)).astype(BF16)
    v_ref[0] = proj(OFF_V, RET_V_WIDTH).astype(BF16)
    gr_ref[0] = proj(OFF_G, RET_V_WIDTH).astype(BF16)
    ga_ref[0] = proj(OFF_GA, D_MODEL).astype(BF16)
    gb_ref[0] = proj(OFF_GB, D_MODEL).astype(BF16)


def _inproj(x, pos128, rope, g, w_bf):
    bsz, seqlen, _ = x.shape
    tm = min(TM_PROJ, seqlen)
    grid = (bsz, seqlen // tm)
    row = lambda b, t: (b, t, 0)
    const = lambda b, t: (0, 0)
    out_shape = (
        jax.ShapeDtypeStruct((seqlen, bsz * S5_WIDTH), BF16),
        jax.ShapeDtypeStruct((bsz, seqlen, RET_QK_WIDTH), BF16),
        jax.ShapeDtypeStruct((bsz, seqlen, RET_QK_WIDTH), BF16),
        jax.ShapeDtypeStruct((bsz, seqlen, RET_V_WIDTH), BF16),
        jax.ShapeDtypeStruct((bsz, seqlen, RET_V_WIDTH), BF16),
        jax.ShapeDtypeStruct((bsz, seqlen, D_MODEL), BF16),
        jax.ShapeDtypeStruct((bsz, seqlen, D_MODEL), BF16),
    )
    return pl.pallas_call(
        _inproj_body,
        grid=grid,
        in_specs=[
            pl.BlockSpec((1, tm, D_MODEL), row),
            pl.BlockSpec((1, tm, LANES), row),
            pl.BlockSpec((2, LANES), const),
            pl.BlockSpec((1, D_MODEL), const),
            pl.BlockSpec((D_MODEL, IN_WIDTH), const, pipeline_mode=pl.Buffered(1)),
        ],
        out_specs=(
            pl.BlockSpec((tm, S5_WIDTH), lambda b, t: (t, b)),
            pl.BlockSpec((1, tm, RET_QK_WIDTH), row),
            pl.BlockSpec((1, tm, RET_QK_WIDTH), row),
            pl.BlockSpec((1, tm, RET_V_WIDTH), row),
            pl.BlockSpec((1, tm, RET_V_WIDTH), row),
            pl.BlockSpec((1, tm, D_MODEL), row),
            pl.BlockSpec((1, tm, D_MODEL), row),
        ),
        out_shape=out_shape,
        compiler_params=pltpu.CompilerParams(
            dimension_semantics=("arbitrary", "arbitrary"), vmem_limit_bytes=VMEM_LIMIT),
        name="inproj",
    )(x, pos128, rope, g, w_bf)


def _s5_body(u_ref, lam_ref, bblk_ref, cblk_ref, dvec_ref, wglu_ref, o_ref,
             bf_ref, a8_ref, p_ref, xre_ref, xim_ref):
    tt = u_ref.shape[0]
    rows = xre_ref.shape[0]
    bsz = rows // tt
    cw = S5_WIDTH // S5_CHUNKS
    cl = S5_LANES // S5_CHUNKS

    @pl.when(pl.program_id(0) == 0)
    def _init():
        lr = lam_ref[0:1, :]
        li = lam_ref[1:2, :]
        dt = jnp.exp(lam_ref[2:3, :])
        mag = jnp.exp(lr * dt)
        a_re = mag * jnp.cos(li * dt)
        a_im = mag * jnp.sin(li * dt)
        nr = a_re - 1.0
        den = lr * lr + li * li
        f_re = (nr * lr + a_im * li) / den
        f_im = (a_im * lr - nr * li) / den
        for c in range(S5_CHUNKS):
            fr = f_re[:, c * cl:(c + 1) * cl]
            fi = f_im[:, c * cl:(c + 1) * cl]
            b_re = bblk_ref[0, c]
            b_im = bblk_ref[1, c]
            bf_ref[0, c] = (b_re * fr - b_im * fi).astype(BF16)
            bf_ref[1, c] = (b_im * fr + b_re * fi).astype(BF16)
        a8_ref[0] = jnp.broadcast_to(a_re, (SUBLANES, S5_LANES))
        a8_ref[1] = jnp.broadcast_to(a_im, (SUBLANES, S5_LANES))
        p_ref[...] = jnp.zeros_like(p_ref)

    r_i = lax.broadcasted_iota(jnp.int32, (rows, tt), 0)
    t_i = lax.broadcasted_iota(jnp.int32, (rows, tt), 1)
    u2 = jnp.zeros((rows, S5_WIDTH), F32)
    for bb in range(bsz):
        spread = (r_i == bsz * t_i + bb).astype(BF16)
        u2 = u2 + jnp.dot(spread, u_ref[:, bb * S5_WIDTH:(bb + 1) * S5_WIDTH],
                          preferred_element_type=F32)
    ub = u2.astype(BF16)
    for c in range(S5_CHUNKS):
        uc = ub[:, c * cw:(c + 1) * cw]
        xre_ref[:, c * cl:(c + 1) * cl] = jnp.dot(uc, bf_ref[0, c], preferred_element_type=F32)
        xim_ref[:, c * cl:(c + 1) * cl] = jnp.dot(uc, bf_ref[1, c], preferred_element_type=F32)

    lower = lax.broadcasted_iota(jnp.int32, (SUBLANES, S5_LC), 0) < (SUBLANES // 2)
    for j in range(S5_LANES // S5_LC):
        sl = slice(j * S5_LC, (j + 1) * S5_LC)
        ar = a8_ref[0, :, sl]
        ai = a8_ref[1, :, sl]

        def step(k, carry, sl=sl, ar=ar, ai=ai):
            pr, pi = carry
            r0 = pl.multiple_of(k * SUBLANES, SUBLANES)
            xr = xre_ref[pl.ds(r0, SUBLANES), sl]
            xi = xim_ref[pl.ds(r0, SUBLANES), sl]
            s1r = ar * pr - ai * pi + xr
            s1i = ar * pi + ai * pr + xi
            tr = pltpu.roll(s1r, SUBLANES // 2, 0)
            ti = pltpu.roll(s1i, SUBLANES // 2, 0)
            s2r = ar * tr - ai * ti + xr
            s2i = ar * ti + ai * tr + xi
            xre_ref[pl.ds(r0, SUBLANES), sl] = jnp.where(lower, s1r, s2r)
            xim_ref[pl.ds(r0, SUBLANES), sl] = jnp.where(lower, s1i, s2i)
            return pltpu.roll(s2r, SUBLANES // 2, 0), pltpu.roll(s2i, SUBLANES // 2, 0)

        pr, pi = lax.fori_loop(0, rows // SUBLANES, step, (p_ref[0, :, sl], p_ref[1, :, sl]),
                               unroll=2)
        p_ref[0, :, sl] = pr
        p_ref[1, :, sl] = pi

    y = jnp.concatenate([
        jnp.dot(xre_ref[:, c * cl:(c + 1) * cl].astype(BF16), cblk_ref[0, c],
                preferred_element_type=F32)
        - jnp.dot(xim_ref[:, c * cl:(c + 1) * cl].astype(BF16), cblk_ref[1, c],
                  preferred_element_type=F32)
        for c in range(S5_CHUNKS)], axis=1)
    y = y + dvec_ref[0:1, :] * u2
    z = jax.nn.gelu(y)
    gate = jax.nn.sigmoid(
        jnp.dot(z.astype(BF16), wglu_ref[...], preferred_element_type=F32) + dvec_ref[1:2, :])
    out = (z * gate).astype(BF16)
    t_o = lax.broadcasted_iota(jnp.int32, (tt, rows), 0)
    r_o = lax.broadcasted_iota(jnp.int32, (tt, rows), 1)
    for bb in range(bsz):
        pick = (r_o == bsz * t_o + bb).astype(BF16)
        o_ref[:, bb * S5_WIDTH:(bb + 1) * S5_WIDTH] = jnp.dot(
            pick, out, preferred_element_type=F32).astype(BF16)


def _s5(u, lam, bblk, cblk, dvec, wglu, bsz):
    seqlen = u.shape[0]
    tt = min(TT_S5, seqlen)
    rows = tt * bsz
    assert rows % SUBLANES == 0 and SUBLANES % bsz == 0
    const2 = lambda i: (0, 0)
    const4 = lambda i: (0, 0, 0, 0)
    cw = S5_WIDTH // S5_CHUNKS
    cl = S5_LANES // S5_CHUNKS
    return pl.pallas_call(
        _s5_body,
        grid=(seqlen // tt,),
        in_specs=[
            pl.BlockSpec((tt, bsz * S5_WIDTH), lambda i: (i, 0)),
            pl.BlockSpec((3, S5_LANES), const2),
            pl.BlockSpec((2, S5_CHUNKS, cw, cl), const4),
            pl.BlockSpec((2, S5_CHUNKS, cl, cw), const4),
            pl.BlockSpec((2, S5_WIDTH), const2),
            pl.BlockSpec((S5_WIDTH, S5_WIDTH), const2),
        ],
        out_specs=pl.BlockSpec((tt, bsz * S5_WIDTH), lambda i: (i, 0)),
        out_shape=jax.ShapeDtypeStruct((seqlen, bsz * S5_WIDTH), BF16),
        scratch_shapes=[
            pltpu.VMEM((2, S5_CHUNKS, cw, cl), BF16),
            pltpu.VMEM((2, SUBLANES, S5_LANES), F32),
            pltpu.VMEM((2, SUBLANES, S5_LANES), F32),
            pltpu.VMEM((rows, S5_LANES), F32),
            pltpu.VMEM((rows, S5_LANES), F32),
        ],
        compiler_params=pltpu.CompilerParams(
            dimension_semantics=("arbitrary",), vmem_limit_bytes=VMEM_LIMIT),
        name="s5_mixer",
    )(u, lam, bblk, cblk, dvec, wglu)


def _ret_body(q_ref, k_ref, v_ref, g_ref, intra_ref, qd_ref, kd_ref, cd_ref, gn_ref, o_ref,
              st_ref):
    tr = q_ref.shape[1]

    @pl.when(pl.program_id(1) == 0)
    def _init():
        st_ref[...] = jnp.zeros_like(st_ref)

    for h in range(RET_HEADS):
        qs = slice(h * RET_DK, (h + 1) * RET_DK)
        vs = slice(h * RET_DV, (h + 1) * RET_DV)
        for c in range(tr // RET_CHUNK):
            rs = slice(c * RET_CHUNK, (c + 1) * RET_CHUNK)
            qc = q_ref[0, rs, qs]
            kc = k_ref[0, rs, qs]
            vc = v_ref[0, rs, vs]
            scores = lax.dot_general(qc, kc, (((1,), (1,)), ((), ())),
                                     preferred_element_type=F32) * intra_ref[h]
            inner = jnp.dot(scores.astype(BF16), vc, preferred_element_type=F32)
            st = st_ref[h]
            qdec = (qc.astype(F32) * qd_ref[h]).astype(BF16)
            cross = jnp.dot(qdec, st.astype(BF16), preferred_element_type=F32)
            kdec_t = (kc.astype(F32) * kd_ref[h]).T.astype(BF16)
            st_ref[h] = cd_ref[h] * st + jnp.dot(kdec_t, vc, preferred_element_type=F32)
            ret = inner + cross
            mu = jnp.mean(ret, axis=-1, keepdims=True)
            dev = ret - mu
            var = jnp.mean(dev * dev, axis=-1, keepdims=True)
            yn = dev * lax.rsqrt(var + NORM_EPS) * gn_ref[0:1, vs]
            o_ref[0, rs, vs] = (jax.nn.silu(g_ref[0, rs, vs].astype(F32)) * yn).astype(BF16)


def _retention(q, k, v, g, intra, qd, kd, cd, gn):
    bsz, seqlen, _ = q.shape
    tr = min(TR_RET, seqlen)
    row = lambda b, t: (b, t, 0)
    const3 = lambda b, t: (0, 0, 0)
    return pl.pallas_call(
        _ret_body,
        grid=(bsz, seqlen // tr),
        in_specs=[
            pl.BlockSpec((1, tr, RET_QK_WIDTH), row),
            pl.BlockSpec((1, tr, RET_QK_WIDTH), row),
            pl.BlockSpec((1, tr, RET_V_WIDTH), row),
            pl.BlockSpec((1, tr, RET_V_WIDTH), row),
            pl.BlockSpec((RET_HEADS, RET_CHUNK, RET_CHUNK), const3),
            pl.BlockSpec((RET_HEADS, RET_CHUNK, RET_DK), const3),
            pl.BlockSpec((RET_HEADS, RET_CHUNK, RET_DK), const3),
            pl.BlockSpec((RET_HEADS, 1, RET_DV), const3),
            pl.BlockSpec((1, RET_V_WIDTH), lambda b, t: (0, 0)),
        ],
        out_specs=pl.BlockSpec((1, tr, RET_V_WIDTH), row),
        out_shape=jax.ShapeDtypeStruct((bsz, seqlen, RET_V_WIDTH), BF16),
        scratch_shapes=[pltpu.VMEM((RET_HEADS, RET_DK, RET_DV), F32)],
        compiler_params=pltpu.CompilerParams(
            dimension_semantics=("arbitrary", "arbitrary"), vmem_limit_bytes=VMEM_LIMIT),
        name="retention",
    )(q, k, v, g, intra, qd, kd, cd, gn)


def _merge_body(x_ref, ys_ref, rg_ref, ga_ref, gb_ref, wbs_ref, wbr_ref, wo_ref, gf_ref,
                rw_ref, rb_ref,
                x1_ref, h2_ref, ri_ref, gt_ref, cnt_ref, run_ref):
    tm = x_ref.shape[1]

    @pl.when((pl.program_id(0) == 0) & (pl.program_id(1) == 0))
    def _init():
        run_ref[...] = jnp.zeros_like(run_ref)

    y_a = jnp.dot(ys_ref[...], wbs_ref[...], preferred_element_type=F32)
    y_b = jnp.dot(rg_ref[0], wbr_ref[...], preferred_element_type=F32)
    merged = (jax.nn.sigmoid(ga_ref[0].astype(F32)) * y_a
              + jax.nn.sigmoid(gb_ref[0].astype(F32)) * y_b)
    x1 = x_ref[0] + jnp.dot(merged.astype(BF16), wo_ref[...], preferred_element_type=F32)
    x1_ref[0] = x1
    h2 = _rms(x1, gf_ref[...])
    h2_ref[0] = h2.reshape(tm, SUBLANES, LANES)

    lane = lax.broadcasted_iota(jnp.int32, (tm, LANES), 1)
    h_hi = h2.astype(BF16)
    h_lo = (h2 - h_hi.astype(F32)).astype(BF16)
    hi_terms = jnp.dot(h_hi, rw_ref[...], preferred_element_type=F32)
    logits = (hi_terms[:, :LANES] + hi_terms[:, LANES:]
              + jnp.dot(h_lo, rw_ref[:, :LANES], preferred_element_type=F32)) + rb_ref[...]
    work = jnp.where(lane < N_EXPERTS, logits, -jnp.inf)
    lane_f = lane.astype(F32)
    vals, hots = [], []
    ri = jnp.zeros((tm, LANES), jnp.int32)
    for kk in range(TOP_K):
        m = jnp.max(work, axis=-1, keepdims=True)
        idx = jnp.min(jnp.where(work == m, lane_f, float(LANES)), axis=-1, keepdims=True)
        hot = lane_f == idx
        vals.append(m)
        hots.append(hot)
        ri = jnp.where(lane == kk, idx.astype(jnp.int32), ri)
        work = jnp.where(hot, -jnp.inf, work)
    exps = [jnp.exp(v - vals[0]) for v in vals]
    denom = exps[0] + exps[1] + exps[2] + exps[3]
    gt = jnp.zeros((tm, LANES), F32)
    for kk in range(TOP_K):
        gt = jnp.where(lane == kk, exps[kk] / denom, gt)
    gt_ref[0] = gt

    onehot = [h.astype(F32) for h in hots]
    tot = onehot[0] + onehot[1] + onehot[2] + onehot[3]
    r_i = lax.broadcasted_iota(jnp.int32, (tm, tm), 0)
    c_i = lax.broadcasted_iota(jnp.int32, (tm, tm), 1)
    tril = (c_i < r_i).astype(BF16)
    base = jnp.dot(tril, tot.astype(BF16), preferred_element_type=F32) + run_ref[...]
    for kk in range(TOP_K):
        rank = jnp.sum(onehot[kk] * base, axis=-1, keepdims=True).astype(jnp.int32)
        ri = jnp.where(lane == TOP_K + kk, rank, ri)
    ri_ref[0] = ri
    run_ref[...] = run_ref[...] + jnp.sum(tot, axis=0, keepdims=True)
    cnt_ref[...] = run_ref[...]


def _merge(x, ys5, retg, ga, gb, wbs, wbr, wo, gf, rw, rb):
    bsz, seqlen, _ = x.shape
    tm = min(TM_PROJ, seqlen)
    row = lambda b, t: (b, t, 0)
    const = lambda b, t: (0, 0)
    out_shape = (
        jax.ShapeDtypeStruct((bsz, seqlen, D_MODEL), F32),
        jax.ShapeDtypeStruct((bsz, seqlen, SUBLANES, LANES), F32),
        jax.ShapeDtypeStruct((bsz, seqlen, LANES), jnp.int32),
        jax.ShapeDtypeStruct((bsz, seqlen, LANES), F32),
        jax.ShapeDtypeStruct((1, LANES), F32),
    )
    return pl.pallas_call(
        _merge_body,
        grid=(bsz, seqlen // tm),
        in_specs=[
            pl.BlockSpec((1, tm, D_MODEL), row),
            pl.BlockSpec((tm, S5_WIDTH), lambda b, t: (t, b)),
            pl.BlockSpec((1, tm, RET_V_WIDTH), row),
            pl.BlockSpec((1, tm, D_MODEL), row),
            pl.BlockSpec((1, tm, D_MODEL), row),
            pl.BlockSpec((S5_WIDTH, D_MODEL), const),
            pl.BlockSpec((RET_V_WIDTH, D_MODEL), const),
            pl.BlockSpec((D_MODEL, D_MODEL), const),
            pl.BlockSpec((1, D_MODEL), const),
            pl.BlockSpec((D_MODEL, 2 * LANES), const),
            pl.BlockSpec((1, LANES), const),
        ],
        out_specs=(
            pl.BlockSpec((1, tm, D_MODEL), row),
            pl.BlockSpec((1, tm, SUBLANES, LANES), lambda b, t: (b, t, 0, 0)),
            pl.BlockSpec((1, tm, LANES), row),
            pl.BlockSpec((1, tm, LANES), row),
            pl.BlockSpec((1, LANES), const),
        ),
        out_shape=out_shape,
        scratch_shapes=[pltpu.VMEM((1, LANES), F32)],
        compiler_params=pltpu.CompilerParams(
            dimension_semantics=("arbitrary", "arbitrary"), vmem_limit_bytes=VMEM_LIMIT),
        name="merge_router",
    )(x, ys5, retg, ga, gb, wbs, wbr, wo, gf, rw, rb)


def _dispatch_body(z_ref, dest_ref, h_ref, xout_ref, zbuf_ref, hbuf_ref, sems, zsem):
    td = h_ref.shape[0]
    bm = zbuf_ref.shape[0]
    n_blocks = xout_ref.shape[0] // bm

    @pl.when(pl.program_id(0) == 0)
    def _zero_fill():
        zbuf_ref[...] = jnp.zeros_like(zbuf_ref)

        def pad_row(e, i):
            return pltpu.make_async_copy(zbuf_ref.at[0], xout_ref.at[z_ref[e] + i], zsem)

        def tail_block(j):
            return pltpu.make_async_copy(
                zbuf_ref, xout_ref.at[pl.ds(pl.multiple_of(j * bm, bm), bm)], zsem)

        def over_fill_copies(fn):
            def per_expert(e, carry):
                def per_row(i, c):
                    fn(pad_row(e, i))
                    return c
                return lax.fori_loop(0, z_ref[N_EXPERTS + e], per_row, carry)

            def per_block(j, c):
                fn(tail_block(j))
                return c

            lax.fori_loop(0, N_EXPERTS, per_expert, 0)
            lax.fori_loop(z_ref[2 * N_EXPERTS], n_blocks, per_block, 0)

        over_fill_copies(lambda cp: cp.start())
        over_fill_copies(lambda cp: cp.wait())

    step = pl.program_id(0)
    last = pl.num_programs(0) - 1

    def drain(which):
        def wait(r, c):
            for kk in range(TOP_K):
                pltpu.make_async_copy(hbuf_ref.at[which, 0], xout_ref.at[0], sems.at[which]).wait()
            return c
        lax.fori_loop(0, td, wait, 0, unroll=8)

    for par in range(2):
        @pl.when(step % 2 == par)
        def _scatter(par=par):
            hbuf_ref[par] = h_ref[...]

            def start(r, c):
                for kk in range(TOP_K):
                    d = dest_ref[0, 0, r * TOP_K + kk]
                    pltpu.make_async_copy(hbuf_ref.at[par, r], xout_ref.at[d],
                                          sems.at[par]).start(priority=kk % 2)
                return c

            lax.fori_loop(0, td, start, 0, unroll=8)

            @pl.when(step > 0)
            def _wait_previous():
                drain(1 - par)

            @pl.when(step == last)
            def _wait_last():
                drain(par)


def _dispatch(zinfo, dest, h2, n_rows, bm):
    n_tok = h2.shape[0]
    td = min(TD_ROWS, n_tok)
    dest3 = dest.reshape(n_tok // td, 1, td * TOP_K)
    grid_spec = pltpu.PrefetchScalarGridSpec(
        num_scalar_prefetch=1,
        grid=(n_tok // td,),
        in_specs=[
            pl.BlockSpec((1, 1, td * TOP_K), lambda i, z: (i, 0, 0), memory_space=pltpu.SMEM),
            pl.BlockSpec((td, SUBLANES, LANES), lambda i, z: (i, 0, 0)),
        ],
        out_specs=pl.BlockSpec(memory_space=pl.ANY),
        scratch_shapes=[pltpu.VMEM((bm, SUBLANES, LANES), F32),
                        pltpu.VMEM((2, td, SUBLANES, LANES), F32),
                        pltpu.SemaphoreType.DMA((2,)), pltpu.SemaphoreType.DMA(())],
    )
    return pl.pallas_call(
        _dispatch_body,
        grid_spec=grid_spec,
        out_shape=jax.ShapeDtypeStruct((n_rows, SUBLANES, LANES), F32),
        compiler_params=pltpu.CompilerParams(
            dimension_semantics=("arbitrary",), has_side_effects=True),
        name="dispatch",
    )(zinfo, dest3, h2)


def _w1prep_body(w_ref, p_ref, o_ref):
    for c in range(D_FF // LANES):
        wc = w_ref[0, :, c * 2 * LANES:(c + 1) * 2 * LANES].astype(BF16)
        r = jnp.dot(wc, p_ref[...], preferred_element_type=F32).astype(BF16)
        o_ref[0, :, c * LANES:(c + 1) * LANES] = r[:, :LANES]
        o_ref[0, :, D_FF + c * LANES:D_FF + (c + 1) * LANES] = r[:, LANES:]


def _w1prep(w1):
    i = lax.broadcasted_iota(jnp.int32, (2 * LANES, 2 * LANES), 0)
    j = lax.broadcasted_iota(jnp.int32, (2 * LANES, 2 * LANES), 1)
    perm = (i == jnp.where(j < LANES, 2 * j, 2 * (j - LANES) + 1)).astype(BF16)
    return pl.pallas_call(
        _w1prep_body,
        grid=(N_EXPERTS,),
        in_specs=[
            pl.BlockSpec((1, D_MODEL, 2 * D_FF), lambda e: (e, 0, 0)),
            pl.BlockSpec((2 * LANES, 2 * LANES), lambda e: (0, 0)),
        ],
        out_specs=pl.BlockSpec((1, D_MODEL, 2 * D_FF), lambda e: (e, 0, 0)),
        out_shape=jax.ShapeDtypeStruct((N_EXPERTS, D_MODEL, 2 * D_FF), BF16),
        compiler_params=pltpu.CompilerParams(
            dimension_semantics=("arbitrary",), vmem_limit_bytes=VMEM_LIMIT),
        name="w1_prep",
    )(w1, perm)


def _expert_body(be_ref, na_ref, x_ref, w1_ref, b1_ref, w2_ref, b2_ref, y_ref):
    del be_ref
    active = pl.program_id(0) < na_ref[0]

    @pl.when(active)
    def _compute():
        bm = x_ref.shape[0]
        xb = x_ref[...].reshape(bm, D_MODEL).astype(BF16)
        gu = jnp.dot(xb, w1_ref[0], preferred_element_type=F32) + b1_ref[0]
        x_glu = jnp.minimum(gu[:, :D_FF], SWIGLU_LIMIT)
        x_lin = jnp.clip(gu[:, D_FF:], -SWIGLU_LIMIT, SWIGLU_LIMIT)
        act = x_glu * jax.nn.sigmoid(SWIGLU_ALPHA * x_glu) * (x_lin + 1.0)
        y = jnp.dot(act.astype(BF16), w2_ref[0], preferred_element_type=F32) + b2_ref[0]
        y_ref[...] = y.reshape(bm, SUBLANES, LANES)

    @pl.when(jnp.logical_not(active))
    def _unused_block():
        y_ref[...] = jnp.zeros_like(y_ref)


def _experts(block_expert, n_active, x_rows, w1, b1, w2, b2, bm):
    n_rows = x_rows.shape[0]
    grid_spec = pltpu.PrefetchScalarGridSpec(
        num_scalar_prefetch=2,
        grid=(n_rows // bm,),
        in_specs=[
            pl.BlockSpec((bm, SUBLANES, LANES), lambda i, be, na: (i, 0, 0)),
            pl.BlockSpec((1, D_MODEL, 2 * D_FF), lambda i, be, na: (be[i], 0, 0)),
            pl.BlockSpec((1, 1, 2 * D_FF), lambda i, be, na: (be[i], 0, 0)),
            pl.BlockSpec((1, D_FF, D_MODEL), lambda i, be, na: (be[i], 0, 0)),
            pl.BlockSpec((1, 1, D_MODEL), lambda i, be, na: (be[i], 0, 0)),
        ],
        out_specs=pl.BlockSpec((bm, SUBLANES, LANES), lambda i, be, na: (i, 0, 0)),
    )
    return pl.pallas_call(
        _expert_body,
        grid_spec=grid_spec,
        out_shape=jax.ShapeDtypeStruct((n_rows, SUBLANES, LANES), F32),
        compiler_params=pltpu.CompilerParams(
            dimension_semantics=("arbitrary",), vmem_limit_bytes=VMEM_LIMIT),
        name="expert_ffn",
    )(block_expert, n_active, x_rows, w1, b1, w2, b2)


def _combine_body(dest_ref, nxt_ref, gt_ref, x1_ref, gfin_ref, yrows_ref, o_ref, ybuf_ref, sems):
    td = x1_ref.shape[0]
    step = pl.program_id(0)
    slot = step % 2

    def issue(idx_ref, which):
        def start(r, c):
            for kk in range(TOP_K):
                d = idx_ref[0, 0, r * TOP_K + kk]
                pltpu.make_async_copy(yrows_ref.at[d], ybuf_ref.at[which, kk, r],
                                      sems.at[which]).start(priority=kk % 2)
            return c
        lax.fori_loop(0, td, start, 0, unroll=8)

    @pl.when(step == 0)
    def _first():
        issue(dest_ref, 0)

    for par in range(2):
        @pl.when((step < pl.num_programs(0) - 1) & (slot == par))
        def _prefetch(par=par):
            issue(nxt_ref, 1 - par)

    def wait(r, c):
        for kk in range(TOP_K):
            pltpu.make_async_copy(yrows_ref.at[0], ybuf_ref.at[0, kk, 0], sems.at[slot]).wait()
        return c

    lax.fori_loop(0, td, wait, 0, unroll=8)
    gt = gt_ref[...]
    acc = x1_ref[...]
    for kk in range(TOP_K):
        acc = acc + gt[:, kk:kk + 1] * ybuf_ref[slot, kk].reshape(td, D_MODEL)
    o_ref[...] = _rms(acc, gfin_ref[...])


def _combine(dest, gates, x1, gfin, y_rows):
    n_tok = x1.shape[0]
    td = min(TD_ROWS, n_tok)
    n_steps = n_tok // td
    dest3 = dest.reshape(n_steps, 1, td * TOP_K)
    return pl.pallas_call(
        _combine_body,
        grid=(n_steps,),
        in_specs=[
            pl.BlockSpec((1, 1, td * TOP_K), lambda i: (i, 0, 0), memory_space=pltpu.SMEM),
            pl.BlockSpec((1, 1, td * TOP_K), lambda i: (jnp.minimum(i + 1, n_steps - 1), 0, 0),
                         memory_space=pltpu.SMEM),
            pl.BlockSpec((td, LANES), lambda i: (i, 0)),
            pl.BlockSpec((td, D_MODEL), lambda i: (i, 0)),
            pl.BlockSpec((1, D_MODEL), lambda i: (0, 0)),
            pl.BlockSpec(memory_space=pl.ANY),
        ],
        out_specs=pl.BlockSpec((td, D_MODEL), lambda i: (i, 0)),
        out_shape=jax.ShapeDtypeStruct((n_tok, D_MODEL), F32),
        scratch_shapes=[pltpu.VMEM((2, TOP_K, td, SUBLANES, LANES), F32),
                        pltpu.SemaphoreType.DMA((2,))],
        compiler_params=pltpu.CompilerParams(
            dimension_semantics=("arbitrary",), vmem_limit_bytes=VMEM_LIMIT),
        name="combine",
    )(dest3, dest3, gates, x1, gfin, y_rows)


def _block_diag(p):
    g, a, b = p.shape
    eye = jnp.eye(g, dtype=p.dtype)
    return (p[:, :, None, :] * eye[:, None, :, None]).reshape(g * a, g * b)


def _retention_tables():
    log_g = jnp.log1p(-(2.0 ** (-5.0 - jnp.arange(RET_HEADS, dtype=F32))))
    pos = jnp.arange(RET_CHUNK, dtype=F32)
    rel = pos[:, None] - pos[None, :]
    intra = jnp.where(rel >= 0, jnp.exp(log_g[:, None, None] * jnp.maximum(rel, 0.0)), 0.0)
    q_decay = jnp.exp(log_g[:, None] * (pos + 1.0))
    k_decay = jnp.exp(log_g[:, None] * (RET_CHUNK - 1.0 - pos))
    chunk_decay = jnp.exp(log_g * RET_CHUNK)
    qd = jnp.broadcast_to(q_decay[:, :, None], (RET_HEADS, RET_CHUNK, RET_DK))
    kd = jnp.broadcast_to(k_decay[:, :, None], (RET_HEADS, RET_CHUNK, RET_DK))
    cd = jnp.broadcast_to(chunk_decay[:, None, None], (RET_HEADS, 1, RET_DV))
    return intra, qd, kd, cd


def _layer(x, positions, norm_mix_g, w_in, lam_re, lam_im, log_dt, b_re, b_im, c_re, c_im,
           s5_d, w_glu, b_glu, ret_gn_g, w_branch_s5, w_branch_ret, w_out, norm_ffn_g,
           router_w, router_b, w1, b1, w2, b2, g_out):
    bsz, seqlen, _ = x.shape
    n_tok = bsz * seqlen

    half = RET_DK // 2
    inv_freq = ROPE_BASE ** (-jnp.arange(half, dtype=F32) / half)
    rope = jnp.stack([jnp.concatenate([inv_freq, inv_freq]),
                      jnp.concatenate([-jnp.ones((half,), F32), jnp.ones((half,), F32)])])
    pos128 = jnp.broadcast_to(positions.astype(F32)[..., None], (bsz, seqlen, LANES))
    u, q, k, v, g_ret, gate_a, gate_b = _inproj(
        x, pos128, rope, norm_mix_g.reshape(1, D_MODEL), w_in.astype(BF16))

    lam = jnp.stack([lam_re.reshape(-1), lam_im.reshape(-1),
                     jnp.repeat(log_dt, S5_STATE)]).astype(F32)
    gpc = S5_GROUPS // S5_CHUNKS

    def chunked_block_diag(p):
        return jax.vmap(_block_diag)(p.reshape(S5_CHUNKS, gpc, p.shape[1], p.shape[2]))

    bblk = jnp.stack([chunked_block_diag(jnp.swapaxes(b_re, 1, 2)),
                      chunked_block_diag(jnp.swapaxes(b_im, 1, 2))])
    cblk = jnp.stack([chunked_block_diag(jnp.swapaxes(c_re, 1, 2)),
                      chunked_block_diag(jnp.swapaxes(c_im, 1, 2))]).astype(BF16)
    dvec = jnp.stack([s5_d, b_glu])
    ys5 = _s5(u, lam, bblk, cblk, dvec, w_glu.astype(BF16), bsz)

    intra, qd, kd, cd = _retention_tables()
    retg = _retention(q, k, v, g_ret, intra, qd, kd, cd, ret_gn_g.reshape(1, RET_V_WIDTH))

    rw_hi = router_w.astype(BF16)
    rw_lo = (router_w - rw_hi.astype(F32)).astype(BF16)
    rw = (jnp.zeros((D_MODEL, 2 * LANES), BF16).at[:, :N_EXPERTS].set(rw_hi)
          .at[:, LANES:LANES + N_EXPERTS].set(rw_lo))
    rb = jnp.zeros((1, LANES), F32).at[0, :N_EXPERTS].set(router_b)
    x1, h2, route, gates, counts = _merge(
        x, ys5, retg, gate_a, gate_b, w_branch_s5.astype(BF16), w_branch_ret.astype(BF16),
        w_out.astype(BF16), norm_ffn_g.reshape(1, D_MODEL), rw, rb)

    bm = BM_EXPERT
    n_assign = n_tok * TOP_K
    n_rows = (n_assign // bm + N_EXPERTS) * bm
    route = route.reshape(n_tok, LANES)
    top_e = route[:, :TOP_K]
    rank = route[:, TOP_K:2 * TOP_K]
    cnt = counts[0, :N_EXPERTS].astype(jnp.int32)
    padded = (cnt + bm - 1) // bm * bm
    pend = jnp.cumsum(padded)
    pstart = pend - padded
    dest = (pstart[top_e] + rank).astype(jnp.int32)
    block_start = jnp.arange(n_rows // bm, dtype=jnp.int32) * bm
    block_expert = jnp.minimum(
        jnp.sum((pend[None, :] <= block_start[:, None]).astype(jnp.int32), axis=1), N_EXPERTS - 1)
    n_active = (pend[N_EXPERTS - 1:] // bm).astype(jnp.int32)
    zinfo = jnp.concatenate([pstart + cnt, padded - cnt, n_active]).astype(jnp.int32)

    x_rows = _dispatch(zinfo, dest, h2.reshape(n_tok, SUBLANES, LANES), n_rows, bm)
    b1p = jnp.concatenate([b1[:, 0::2], b1[:, 1::2]], axis=-1).reshape(N_EXPERTS, 1, 2 * D_FF)
    y_rows = _experts(block_expert, n_active, x_rows, _w1prep(w1), b1p, w2.astype(BF16),
                      b2.reshape(N_EXPERTS, 1, D_MODEL), bm)
    out = _combine(dest, gates.reshape(n_tok, LANES), x1.reshape(n_tok, D_MODEL),
                   g_out.reshape(1, D_MODEL), y_rows)
    return out.reshape(bsz, seqlen, D_MODEL)


def kernel(x, positions, norm_mix_g, w_in, s5_lambda_re, s5_lambda_im, s5_log_dt, s5_b_re, s5_b_im, s5_c_re, s5_c_im, s5_d, s5_w_glu, s5_b_glu, ret_gn_g, w_branch_s5, w_branch_ret, w_out, norm_ffn_g, router_w, router_b, expert_w1, expert_b1, expert_w2, expert_b2, norm_final_g):
    assert norm_mix_g.shape[0] == 1, "single-layer trunk"
    return _layer(x, positions, norm_mix_g[0], w_in[0], s5_lambda_re[0], s5_lambda_im[0],
                  s5_log_dt[0], s5_b_re[0], s5_b_im[0], s5_c_re[0], s5_c_im[0], s5_d[0],
                  s5_w_glu[0], s5_b_glu[0], ret_gn_g[0], w_branch_s5[0], w_branch_ret[0],
                  w_out[0], norm_ffn_g[0], router_w[0], router_b[0], expert_w1[0], expert_b1[0],
                  expert_w2[0], expert_b2[0], norm_final_g)
```

```python
import jax
import jax.numpy as jnp
from jax import lax
from jax.experimental import pallas as pl
from jax.experimental.pallas import tpu as pltpu

F32 = jnp.float32
BF16 = jnp.bfloat16

D_MODEL = 1024
S5_WIDTH = 512
S5_GROUP = 16
S5_GROUPS = 32
S5_STATE = 64
S5_LANES = S5_GROUPS * S5_STATE
RET_HEADS = 4
RET_DK = 128
RET_DV = 256
RET_QK_WIDTH = RET_HEADS * RET_DK
RET_V_WIDTH = RET_HEADS * RET_DV
RET_CHUNK = 128
ROPE_BASE = 10000.0
N_EXPERTS = 32
TOP_K = 4
D_FF = 1024
SWIGLU_ALPHA = 1.702
SWIGLU_LIMIT = 7.0
NORM_EPS = 1e-5
IN_WIDTH = S5_WIDTH + 2 * RET_QK_WIDTH + 2 * RET_V_WIDTH + 2 * D_MODEL
OFF_Q = S5_WIDTH
OFF_V = OFF_Q + 2 * RET_QK_WIDTH
OFF_G = OFF_V + RET_V_WIDTH
OFF_GA = OFF_G + RET_V_WIDTH
OFF_GB = OFF_GA + D_MODEL

LANES = 128
SUBLANES = 8
VMEM_LIMIT = 56 * 1024 * 1024

TM_PROJ = 512
TT_S5 = 128
S5_LC = 512
S5_CHUNKS = 4
TR_RET = 512
BM_EXPERT = 512
TD_ROWS = 512
MERGE_SPLIT = 1


def _rms(x, g):
    return x * lax.rsqrt(jnp.mean(x * x, axis=-1, keepdims=True) + NORM_EPS) * g


def _sigmoid(x):
    return 0.5 * jnp.tanh(0.5 * x) + 0.5


def _inproj_body(x_ref, pos_ref, rope_ref, g_ref, w_ref,
                 u_ref, q_ref, k_ref, v_ref, gr_ref, ga_ref, gb_ref):
    hb = _rms(x_ref[0], g_ref[...]).astype(BF16)

    def proj(lo, width):
        return jnp.dot(hb, w_ref[:, lo:lo + width], preferred_element_type=F32)

    u_ref[...] = proj(0, S5_WIDTH).astype(BF16)
    ang = pos_ref[0] * rope_ref[0:1, :]
    cos = jnp.cos(ang)
    sin = jnp.sin(ang) * rope_ref[1:2, :]
    qk = proj(OFF_Q, 2 * RET_QK_WIDTH)
    for h in range(RET_HEADS):
        qh = qk[:, h * RET_DK:(h + 1) * RET_DK]
        q_ref[0, :, h * RET_DK:(h + 1) * RET_DK] = (
            qh * cos + pltpu.roll(qh, RET_DK // 2, 1) * sin).astype(BF16)
        kh = qk[:, RET_QK_WIDTH + h * RET_DK:RET_QK_WIDTH + (h + 1) * RET_DK]
        k_ref[0, :, h * RET_DK:(h + 1) * RET_DK] = (
            (kh * cos + pltpu.roll(kh, RET_DK // 2, 1) * sin) * (RET_DK ** -0.5)).astype(BF16)
    v_ref[0] = proj(OFF_V, RET_V_WIDTH).astype(BF16)
    gr_ref[0] = proj(OFF_G, RET_V_WIDTH).astype(BF16)
    ga_ref[0] = proj(OFF_GA, D_MODEL).astype(BF16)
    gb_ref[0] = proj(OFF_GB, D_MODEL).astype(BF16)


def _inproj(x, pos128, rope, g, w_bf):
    bsz, seqlen, _ = x.shape
    tm = min(TM_PROJ, seqlen)
    grid = (bsz, seqlen // tm)
    row = lambda b, t: (b, t, 0)
    const = lambda b, t: (0, 0)
    out_shape = (
        jax.ShapeDtypeStruct((seqlen, bsz * S5_WIDTH), BF16),
        jax.ShapeDtypeStruct((bsz, seqlen, RET_QK_WIDTH), BF16),
        jax.ShapeDtypeStruct((bsz, seqlen, RET_QK_WIDTH), BF16),
        jax.ShapeDtypeStruct((bsz, seqlen, RET_V_WIDTH), BF16),
        jax.ShapeDtypeStruct((bsz, seqlen, RET_V_WIDTH), BF16),
        jax.ShapeDtypeStruct((bsz, seqlen, D_MODEL), BF16),
        jax.ShapeDtypeStruct((bsz, seqlen, D_MODEL), BF16),
    )
    return pl.pallas_call(
        _inproj_body,
        grid=grid,
        in_specs=[
            pl.BlockSpec((1, tm, D_MODEL), row),
            pl.BlockSpec((1, tm, LANES), row),
            pl.BlockSpec((2, LANES), const),
            pl.BlockSpec((1, D_MODEL), const),
            pl.BlockSpec((D_MODEL, IN_WIDTH), const, pipeline_mode=pl.Buffered(1)),
        ],
        out_specs=(
            pl.BlockSpec((tm, S5_WIDTH), lambda b, t: (t, b)),
            pl.BlockSpec((1, tm, RET_QK_WIDTH), row),
            pl.BlockSpec((1, tm, RET_QK_WIDTH), row),
            pl.BlockSpec((1, tm, RET_V_WIDTH), row),
            pl.BlockSpec((1, tm, RET_V_WIDTH), row),
            pl.BlockSpec((1, tm, D_MODEL), row),
            pl.BlockSpec((1, tm, D_MODEL), row),
        ),
        out_shape=out_shape,
        compiler_params=pltpu.CompilerParams(
            dimension_semantics=("arbitrary", "arbitrary"), vmem_limit_bytes=VMEM_LIMIT),
        name="inproj",
    )(x, pos128, rope, g, w_bf)


def _s5_body(u_ref, lam_ref, bblk_ref, cblk_ref, dvec_ref, wglu_ref, o_ref,
             bf_ref, a8_ref, p_ref, xre_ref, xim_ref):
    tt = u_ref.shape[0]
    rows = xre_ref.shape[0]
    bsz = rows // tt
    cw = S5_WIDTH // S5_CHUNKS
    cl = S5_LANES // S5_CHUNKS

    @pl.when(pl.program_id(0) == 0)
    def _init():
        lr = lam_ref[0:1, :]
        li = lam_ref[1:2, :]
        dt = jnp.exp(lam_ref[2:3, :])
        mag = jnp.exp(lr * dt)
        a_re = mag * jnp.cos(li * dt)
        a_im = mag * jnp.sin(li * dt)
        nr = a_re - 1.0
        den = lr * lr + li * li
        f_re = (nr * lr + a_im * li) / den
        f_im = (a_im * lr - nr * li) / den
        for c in range(S5_CHUNKS):
            fr = f_re[:, c * cl:(c + 1) * cl]
            fi = f_im[:, c * cl:(c + 1) * cl]
            b_re = bblk_ref[0, c]
            b_im = bblk_ref[1, c]
            bf_ref[0, c] = (b_re * fr - b_im * fi).astype(BF16)
            bf_ref[1, c] = (b_im * fr + b_re * fi).astype(BF16)
        a8_ref[0] = jnp.broadcast_to(a_re, (SUBLANES, S5_LANES))
        a8_ref[1] = jnp.broadcast_to(a_im, (SUBLANES, S5_LANES))
        p_ref[...] = jnp.zeros_like(p_ref)

    r_i = lax.broadcasted_iota(jnp.int32, (rows, tt), 0)
    t_i = lax.broadcasted_iota(jnp.int32, (rows, tt), 1)
    u2 = jnp.zeros((rows, S5_WIDTH), F32)
    for bb in range(bsz):
        spread = (r_i == bsz * t_i + bb).astype(BF16)
        u2 = u2 + jnp.dot(spread, u_ref[:, bb * S5_WIDTH:(bb + 1) * S5_WIDTH],
                          preferred_element_type=F32)
    ub = u2.astype(BF16)
    for c in range(S5_CHUNKS):
        uc = ub[:, c * cw:(c + 1) * cw]
        xre_ref[:, c * cl:(c + 1) * cl] = jnp.dot(uc, bf_ref[0, c], preferred_element_type=F32)
        xim_ref[:, c * cl:(c + 1) * cl] = jnp.dot(uc, bf_ref[1, c], preferred_element_type=F32)

    lower = lax.broadcasted_iota(jnp.int32, (SUBLANES, S5_LC), 0) < (SUBLANES // 2)
    for j in range(S5_LANES // S5_LC):
        sl = slice(j * S5_LC, (j + 1) * S5_LC)
        ar = a8_ref[0, :, sl]
        ai = a8_ref[1, :, sl]

        def step(k, carry, sl=sl, ar=ar, ai=ai):
            pr, pi = carry
            r0 = pl.multiple_of(k * SUBLANES, SUBLANES)
            xr = xre_ref[pl.ds(r0, SUBLANES), sl]
            xi = xim_ref[pl.ds(r0, SUBLANES), sl]
            s1r = ar * pr - ai * pi + xr
            s1i = ar * pi + ai * pr + xi
            tr = pltpu.roll(s1r, SUBLANES // 2, 0)
            ti = pltpu.roll(s1i, SUBLANES // 2, 0)
            s2r = ar * tr - ai * ti + xr
            s2i = ar * ti + ai * tr + xi
            xre_ref[pl.ds(r0, SUBLANES), sl] = jnp.where(lower, s1r, s2r)
            xim_ref[pl.ds(r0, SUBLANES), sl] = jnp.where(lower, s1i, s2i)
            return pltpu.roll(s2r, SUBLANES // 2, 0), pltpu.roll(s2i, SUBLANES // 2, 0)

        pr, pi = lax.fori_loop(0, rows // SUBLANES, step, (p_ref[0, :, sl], p_ref[1, :, sl]),
                               unroll=2)
        p_ref[0, :, sl] = pr
        p_ref[1, :, sl] = pi

    y = jnp.concatenate([
        jnp.dot(xre_ref[:, c * cl:(c + 1) * cl].astype(BF16), cblk_ref[0, c],
                preferred_element_type=F32)
        - jnp.dot(xim_ref[:, c * cl:(c + 1) * cl].astype(BF16), cblk_ref[1, c],
                  preferred_element_type=F32)
        for c in range(S5_CHUNKS)], axis=1)
    y = y + dvec_ref[0:1, :] * u2
    z = jax.nn.gelu(y)
    gate = jax.nn.sigmoid(
        jnp.dot(z.astype(BF16), wglu_ref[...], preferred_element_type=F32) + dvec_ref[1:2, :])
    out = (z * gate).astype(BF16)
    t_o = lax.broadcasted_iota(jnp.int32, (tt, rows), 0)
    r_o = lax.broadcasted_iota(jnp.int32, (tt, rows), 1)
    for bb in range(bsz):
        pick = (r_o == bsz * t_o + bb).astype(BF16)
        o_ref[:, bb * S5_WIDTH:(bb + 1) * S5_WIDTH] = jnp.dot(
            pick, out, preferred_element_type=F32).astype(BF16)


def _s5(u, lam, bblk, cblk, dvec, wglu, bsz):
    seqlen = u.shape[0]
    tt = min(TT_S5, seqlen)
    rows = tt * bsz
    assert rows % SUBLANES == 0 and SUBLANES % bsz == 0
    const2 = lambda i: (0, 0)
    const4 = lambda i: (0, 0, 0, 0)
    cw = S5_WIDTH // S5_CHUNKS
    cl = S5_LANES // S5_CHUNKS
    return pl.pallas_call(
        _s5_body,
        grid=(seqlen // tt,),
        in_specs=[
            pl.BlockSpec((tt, bsz * S5_WIDTH), lambda i: (i, 0)),
            pl.BlockSpec((3, S5_LANES), const2),
            pl.BlockSpec((2, S5_CHUNKS, cw, cl), const4),
            pl.BlockSpec((2, S5_CHUNKS, cl, cw), const4),
            pl.BlockSpec((2, S5_WIDTH), const2),
            pl.BlockSpec((S5_WIDTH, S5_WIDTH), const2),
        ],
        out_specs=pl.BlockSpec((tt, bsz * S5_WIDTH), lambda i: (i, 0)),
        out_shape=jax.ShapeDtypeStruct((seqlen, bsz * S5_WIDTH), BF16),
        scratch_shapes=[
            pltpu.VMEM((2, S5_CHUNKS, cw, cl), BF16),
            pltpu.VMEM((2, SUBLANES, S5_LANES), F32),
            pltpu.VMEM((2, SUBLANES, S5_LANES), F32),
            pltpu.VMEM((rows, S5_LANES), F32),
            pltpu.VMEM((rows, S5_LANES), F32),
        ],
        compiler_params=pltpu.CompilerParams(
            dimension_semantics=("arbitrary",), vmem_limit_bytes=VMEM_LIMIT),
        name="s5_mixer",
    )(u, lam, bblk, cblk, dvec, wglu)


def _ret_body(q_ref, k_ref, v_ref, g_ref, intra_ref, qd_ref, kd_ref, cd_ref, gn_ref, o_ref,
              st_ref):
    tr = q_ref.shape[1]

    @pl.when(pl.program_id(1) == 0)
    def _init():
        st_ref[...] = jnp.zeros_like(st_ref)

    for h in range(RET_HEADS):
        qs = slice(h * RET_DK, (h + 1) * RET_DK)
        vs = slice(h * RET_DV, (h + 1) * RET_DV)
        for c in range(tr // RET_CHUNK):
            rs = slice(c * RET_CHUNK, (c + 1) * RET_CHUNK)
            qc = q_ref[0, rs, qs]
            kc = k_ref[0, rs, qs]
            vc = v_ref[0, rs, vs]
            scores = lax.dot_general(qc, kc, (((1,), (1,)), ((), ())),
                                     preferred_element_type=F32) * intra_ref[h]
            inner = jnp.dot(scores.astype(BF16), vc, preferred_element_type=F32)
            st = st_ref[h]
            qdec = (qc.astype(F32) * qd_ref[h]).astype(BF16)
            cross = jnp.dot(qdec, st.astype(BF16), preferred_element_type=F32)
            kdec_t = (kc.astype(F32) * kd_ref[h]).T.astype(BF16)
            st_ref[h] = cd_ref[h] * st + jnp.dot(kdec_t, vc, preferred_element_type=F32)
            ret = inner + cross
            mu = jnp.mean(ret, axis=-1, keepdims=True)
            dev = ret - mu
            var = jnp.mean(dev * dev, axis=-1, keepdims=True)
            yn = dev * lax.rsqrt(var + NORM_EPS) * gn_ref[0:1, vs]
            o_ref[0, rs, vs] = (jax.nn.silu(g_ref[0, rs, vs].astype(F32)) * yn).astype(BF16)


def _retention(q, k, v, g, intra, qd, kd, cd, gn):
    bsz, seqlen, _ = q.shape
    tr = min(TR_RET, seqlen)
    row = lambda b, t: (b, t, 0)
    const3 = lambda b, t: (0, 0, 0)
    return pl.pallas_call(
        _ret_body,
        grid=(bsz, seqlen // tr),
        in_specs=[
            pl.BlockSpec((1, tr, RET_QK_WIDTH), row),
            pl.BlockSpec((1, tr, RET_QK_WIDTH), row),
            pl.BlockSpec((1, tr, RET_V_WIDTH), row),
            pl.BlockSpec((1, tr, RET_V_WIDTH), row),
            pl.BlockSpec((RET_HEADS, RET_CHUNK, RET_CHUNK), const3),
            pl.BlockSpec((RET_HEADS, RET_CHUNK, RET_DK), const3),
            pl.BlockSpec((RET_HEADS, RET_CHUNK, RET_DK), const3),
            pl.BlockSpec((RET_HEADS, 1, RET_DV), const3),
            pl.BlockSpec((1, RET_V_WIDTH), lambda b, t: (0, 0)),
        ],
        out_specs=pl.BlockSpec((1, tr, RET_V_WIDTH), row),
        out_shape=jax.ShapeDtypeStruct((bsz, seqlen, RET_V_WIDTH), BF16),
        scratch_shapes=[pltpu.VMEM((RET_HEADS, RET_DK, RET_DV), F32)],
        compiler_params=pltpu.CompilerParams(
            dimension_semantics=("arbitrary", "arbitrary"), vmem_limit_bytes=VMEM_LIMIT),
        name="retention",
    )(q, k, v, g, intra, qd, kd, cd, gn)


def _merge_body(x_ref, ys_ref, rg_ref, ga_ref, gb_ref, wbs_ref, wbr_ref, wo_ref, gf_ref,
                rw_ref, rb_ref,
                x1_ref, h2_ref, ri_ref, gt_ref, cnt_ref, run_ref):
    tm = x_ref.shape[1]

    @pl.when((pl.program_id(0) == 0) & (pl.program_id(1) == 0))
    def _init():
        run_ref[...] = jnp.zeros_like(run_ref)

    ts = tm // MERGE_SPLIT
    lane = lax.broadcasted_iota(jnp.int32, (ts, LANES), 1)
    lane_f = lane.astype(F32)
    r_i = lax.broadcasted_iota(jnp.int32, (ts, ts), 0)
    c_i = lax.broadcasted_iota(jnp.int32, (ts, ts), 1)
    tril = (c_i < r_i).astype(BF16)
    running = run_ref[...]
    for part in range(MERGE_SPLIT):
        rs = slice(part * ts, (part + 1) * ts)
        y_a = jnp.dot(ys_ref[rs, :], wbs_ref[...], preferred_element_type=F32)
        y_b = jnp.dot(rg_ref[0, rs, :], wbr_ref[...], preferred_element_type=F32)
        merged = (_sigmoid(ga_ref[0, rs, :].astype(F32)) * y_a
                  + _sigmoid(gb_ref[0, rs, :].astype(F32)) * y_b)
        x1 = x_ref[0, rs, :] + jnp.dot(merged.astype(BF16), wo_ref[...],
                                       preferred_element_type=F32)
        x1_ref[0, rs, :] = x1
        h2 = _rms(x1, gf_ref[...])
        h2_ref[0, rs] = h2.reshape(ts, SUBLANES, LANES)

        h_hi = h2.astype(BF16)
        h_lo = (h2 - h_hi.astype(F32)).astype(BF16)
        hi_terms = jnp.dot(h_hi, rw_ref[...], preferred_element_type=F32)
        logits = (hi_terms[:, :LANES] + hi_terms[:, LANES:]
                  + jnp.dot(h_lo, rw_ref[:, :LANES], preferred_element_type=F32)) + rb_ref[...]
        work = jnp.where(lane < N_EXPERTS, logits, -jnp.inf)
        vals, hots = [], []
        ri = jnp.zeros((ts, LANES), jnp.int32)
        for kk in range(TOP_K):
            m = jnp.max(work, axis=-1, keepdims=True)
            idx = jnp.min(jnp.where(work == m, lane_f, float(LANES)), axis=-1, keepdims=True)
            hot = lane_f == idx
            vals.append(m)
            hots.append(hot)
            ri = jnp.where(lane == kk, idx.astype(jnp.int32), ri)
            work = jnp.where(hot, -jnp.inf, work)
        exps = [jnp.exp(v - vals[0]) for v in vals]
        denom = exps[0] + exps[1] + exps[2] + exps[3]
        gt = jnp.zeros((ts, LANES), F32)
        for kk in range(TOP_K):
            gt = jnp.where(lane == kk, exps[kk] / denom, gt)
        gt_ref[0, rs, :] = gt[:, :2 * TOP_K]

        onehot = [h.astype(F32) for h in hots]
        tot = onehot[0] + onehot[1] + onehot[2] + onehot[3]
        base = jnp.dot(tril, tot.astype(BF16), preferred_element_type=F32) + running
        for kk in range(TOP_K):
            rank = jnp.sum(onehot[kk] * base, axis=-1, keepdims=True).astype(jnp.int32)
            ri = jnp.where(lane == TOP_K + kk, rank, ri)
        ri_ref[0, rs, :] = ri[:, :2 * TOP_K]
        running = running + jnp.sum(tot, axis=0, keepdims=True)
    run_ref[...] = running
    cnt_ref[...] = running


def _merge(x, ys5, retg, ga, gb, wbs, wbr, wo, gf, rw, rb):
    bsz, seqlen, _ = x.shape
    tm = min(TM_PROJ, seqlen)
    row = lambda b, t: (b, t, 0)
    const = lambda b, t: (0, 0)
    out_shape = (
        jax.ShapeDtypeStruct((bsz, seqlen, D_MODEL), F32),
        jax.ShapeDtypeStruct((bsz, seqlen, SUBLANES, LANES), F32),
        jax.ShapeDtypeStruct((bsz, seqlen, 2 * TOP_K), jnp.int32),
        jax.ShapeDtypeStruct((bsz, seqlen, 2 * TOP_K), F32),
        jax.ShapeDtypeStruct((1, LANES), F32),
    )
    return pl.pallas_call(
        _merge_body,
        grid=(bsz, seqlen // tm),
        in_specs=[
            pl.BlockSpec((1, tm, D_MODEL), row),
            pl.BlockSpec((tm, S5_WIDTH), lambda b, t: (t, b)),
            pl.BlockSpec((1, tm, RET_V_WIDTH), row),
            pl.BlockSpec((1, tm, D_MODEL), row),
            pl.BlockSpec((1, tm, D_MODEL), row),
            pl.BlockSpec((S5_WIDTH, D_MODEL), const),
            pl.BlockSpec((RET_V_WIDTH, D_MODEL), const),
            pl.BlockSpec((D_MODEL, D_MODEL), const),
            pl.BlockSpec((1, D_MODEL), const),
            pl.BlockSpec((D_MODEL, 2 * LANES), const),
            pl.BlockSpec((1, LANES), const),
        ],
        out_specs=(
            pl.BlockSpec((1, tm, D_MODEL), row),
            pl.BlockSpec((1, tm, SUBLANES, LANES), lambda b, t: (b, t, 0, 0)),
            pl.BlockSpec((1, tm, 2 * TOP_K), row),
            pl.BlockSpec((1, tm, 2 * TOP_K), row),
            pl.BlockSpec((1, LANES), const),
        ),
        out_shape=out_shape,
        scratch_shapes=[pltpu.VMEM((1, LANES), F32)],
        compiler_params=pltpu.CompilerParams(
            dimension_semantics=("arbitrary", "arbitrary"), vmem_limit_bytes=VMEM_LIMIT),
        name="merge_router",
    )(x, ys5, retg, ga, gb, wbs, wbr, wo, gf, rw, rb)


def _dispatch_body(z_ref, dest_ref, h_ref, w1_ref, perm_ref, w2_ref,
                   xout_ref, w1o_ref, w2o_ref, zbuf_ref, hbuf_ref, sems, zsem):
    td = h_ref.shape[0]
    bm = zbuf_ref.shape[0]
    n_blocks = xout_ref.shape[0] // bm

    for c in range(D_FF // LANES):
        wc = w1_ref[:, c * 2 * LANES:(c + 1) * 2 * LANES].astype(BF16)
        r = jnp.dot(wc, perm_ref[...], preferred_element_type=F32).astype(BF16)
        w1o_ref[:, c * LANES:(c + 1) * LANES] = r[:, :LANES]
        w1o_ref[:, D_FF + c * LANES:D_FF + (c + 1) * LANES] = r[:, LANES:]
    w2o_ref[...] = w2_ref[...].astype(BF16)

    @pl.when(pl.program_id(0) == 0)
    def _zero_fill():
        zbuf_ref[...] = jnp.zeros_like(zbuf_ref)

        def pad_row(e, i):
            return pltpu.make_async_copy(zbuf_ref.at[0], xout_ref.at[z_ref[e] + i], zsem)

        def tail_block(j):
            return pltpu.make_async_copy(
                zbuf_ref, xout_ref.at[pl.ds(pl.multiple_of(j * bm, bm), bm)], zsem)

        def over_fill_copies(fn):
            def per_expert(e, carry):
                def per_row(i, c):
                    fn(pad_row(e, i))
                    return c
                return lax.fori_loop(0, z_ref[N_EXPERTS + e], per_row, carry)

            def per_block(j, c):
                fn(tail_block(j))
                return c

            lax.fori_loop(0, N_EXPERTS, per_expert, 0)
            lax.fori_loop(z_ref[2 * N_EXPERTS], n_blocks, per_block, 0)

        over_fill_copies(lambda cp: cp.start())
        over_fill_copies(lambda cp: cp.wait())

    step = pl.program_id(0)
    last = pl.num_programs(0) - 1

    def drain(which):
        def wait(r, c):
            for kk in range(TOP_K):
                pltpu.make_async_copy(hbuf_ref.at[which, 0], xout_ref.at[0], sems.at[which]).wait()
            return c
        lax.fori_loop(0, td, wait, 0, unroll=8)

    for par in range(2):
        @pl.when(step % 2 == par)
        def _scatter(par=par):
            hbuf_ref[par] = h_ref[...]

            def start(r, c):
                for kk in range(TOP_K):
                    d = dest_ref[0, 0, r * TOP_K + kk]
                    pltpu.make_async_copy(hbuf_ref.at[par, r], xout_ref.at[d],
                                          sems.at[par]).start(priority=kk % 2)
                return c

            lax.fori_loop(0, td, start, 0, unroll=8)

            @pl.when(step > 0)
            def _wait_previous():
                drain(1 - par)

            @pl.when(step == last)
            def _wait_last():
                drain(par)


def _dispatch(zinfo, dest, h2, w1, w2, n_rows, bm):
    n_tok = h2.shape[0]
    n_steps = 2 * N_EXPERTS
    td = n_tok // n_steps
    assert td * n_steps == n_tok and td % SUBLANES == 0
    kr = N_EXPERTS * D_MODEL // n_steps
    dest3 = dest.reshape(n_steps, 1, td * TOP_K)
    i_p = lax.broadcasted_iota(jnp.int32, (2 * LANES, 2 * LANES), 0)
    j_p = lax.broadcasted_iota(jnp.int32, (2 * LANES, 2 * LANES), 1)
    perm = (i_p == jnp.where(j_p < LANES, 2 * j_p, 2 * (j_p - LANES) + 1)).astype(BF16)
    grid_spec = pltpu.PrefetchScalarGridSpec(
        num_scalar_prefetch=1,
        grid=(n_steps,),
        in_specs=[
            pl.BlockSpec((1, 1, td * TOP_K), lambda i, z: (i, 0, 0), memory_space=pltpu.SMEM),
            pl.BlockSpec((td, SUBLANES, LANES), lambda i, z: (i, 0, 0)),
            pl.BlockSpec((kr, 2 * D_FF), lambda i, z: (i, 0)),
            pl.BlockSpec((2 * LANES, 2 * LANES), lambda i, z: (0, 0)),
            pl.BlockSpec((kr, D_MODEL), lambda i, z: (i, 0)),
        ],
        out_specs=(
            pl.BlockSpec(memory_space=pl.ANY),
            pl.BlockSpec((kr, 2 * D_FF), lambda i, z: (i, 0)),
            pl.BlockSpec((kr, D_MODEL), lambda i, z: (i, 0)),
        ),
        scratch_shapes=[pltpu.VMEM((bm, SUBLANES, LANES), F32),
                        pltpu.VMEM((2, td, SUBLANES, LANES), F32),
                        pltpu.SemaphoreType.DMA((2,)), pltpu.SemaphoreType.DMA(())],
    )
    x_rows, w1p, w2b = pl.pallas_call(
        _dispatch_body,
        grid_spec=grid_spec,
        out_shape=(
            jax.ShapeDtypeStruct((n_rows, SUBLANES, LANES), F32),
            jax.ShapeDtypeStruct((N_EXPERTS * D_MODEL, 2 * D_FF), BF16),
            jax.ShapeDtypeStruct((N_EXPERTS * D_FF, D_MODEL), BF16),
        ),
        compiler_params=pltpu.CompilerParams(
            dimension_semantics=("arbitrary",), has_side_effects=True,
            vmem_limit_bytes=VMEM_LIMIT),
        name="dispatch",
    )(zinfo, dest3, h2, w1.reshape(N_EXPERTS * D_MODEL, 2 * D_FF), perm,
      w2.reshape(N_EXPERTS * D_FF, D_MODEL))
    return (x_rows, w1p.reshape(N_EXPERTS, D_MODEL, 2 * D_FF),
            w2b.reshape(N_EXPERTS, D_FF, D_MODEL))


def _expert_body(be_ref, na_ref, x_ref, w1_ref, b1_ref, w2_ref, b2_ref, y_ref):
    del be_ref
    active = pl.program_id(0) < na_ref[0]

    @pl.when(active)
    def _compute():
        bm = x_ref.shape[0]
        xb = x_ref[...].reshape(bm, D_MODEL).astype(BF16)
        gu = jnp.dot(xb, w1_ref[0], preferred_element_type=F32) + b1_ref[0]
        x_glu = jnp.minimum(gu[:, :D_FF], SWIGLU_LIMIT)
        x_lin = jnp.clip(gu[:, D_FF:], -SWIGLU_LIMIT, SWIGLU_LIMIT)
        act = x_glu * jax.nn.sigmoid(SWIGLU_ALPHA * x_glu) * (x_lin + 1.0)
        y = jnp.dot(act.astype(BF16), w2_ref[0], preferred_element_type=F32) + b2_ref[0]
        y_ref[...] = y.reshape(bm, SUBLANES, LANES)

    @pl.when(jnp.logical_not(active))
    def _unused_block():
        y_ref[...] = jnp.zeros_like(y_ref)


def _experts(block_expert, n_active, x_rows, w1, b1, w2, b2, bm):
    n_rows = x_rows.shape[0]
    grid_spec = pltpu.PrefetchScalarGridSpec(
        num_scalar_prefetch=2,
        grid=(n_rows // bm,),
        in_specs=[
            pl.BlockSpec((bm, SUBLANES, LANES), lambda i, be, na: (i, 0, 0)),
            pl.BlockSpec((1, D_MODEL, 2 * D_FF), lambda i, be, na: (be[i], 0, 0)),
            pl.BlockSpec((1, 1, 2 * D_FF), lambda i, be, na: (be[i], 0, 0)),
            pl.BlockSpec((1, D_FF, D_MODEL), lambda i, be, na: (be[i], 0, 0)),
            pl.BlockSpec((1, 1, D_MODEL), lambda i, be, na: (be[i], 0, 0)),
        ],
        out_specs=pl.BlockSpec((bm, SUBLANES, LANES), lambda i, be, na: (i, 0, 0)),
    )
    return pl.pallas_call(
        _expert_body,
        grid_spec=grid_spec,
        out_shape=jax.ShapeDtypeStruct((n_rows, SUBLANES, LANES), F32),
        compiler_params=pltpu.CompilerParams(
            dimension_semantics=("arbitrary",), vmem_limit_bytes=VMEM_LIMIT),
        name="expert_ffn",
    )(block_expert, n_active, x_rows, w1, b1, w2, b2)


def _combine_body(dest_ref, nxt_ref, gt_ref, x1_ref, gfin_ref, yrows_ref, o_ref, ybuf_ref, sems):
    td = x1_ref.shape[0]
    step = pl.program_id(0)
    slot = step % 2

    def issue(idx_ref, which):
        def start(r, c):
            for kk in range(TOP_K):
                d = idx_ref[0, 0, r * TOP_K + kk]
                pltpu.make_async_copy(yrows_ref.at[d], ybuf_ref.at[which, kk, r],
                                      sems.at[which]).start(priority=kk % 2)
            return c
        lax.fori_loop(0, td, start, 0, unroll=8)

    @pl.when(step == 0)
    def _first():
        issue(dest_ref, 0)

    for par in range(2):
        @pl.when((step < pl.num_programs(0) - 1) & (slot == par))
        def _prefetch(par=par):
            issue(nxt_ref, 1 - par)

    def wait(r, c):
        for kk in range(TOP_K):
            pltpu.make_async_copy(yrows_ref.at[0], ybuf_ref.at[0, kk, 0], sems.at[slot]).wait()
        return c

    lax.fori_loop(0, td, wait, 0, unroll=8)
    gt = gt_ref[...]
    acc = x1_ref[...]
    for kk in range(TOP_K):
        acc = acc + gt[:, kk:kk + 1] * ybuf_ref[slot, kk].reshape(td, D_MODEL)
    o_ref[...] = _rms(acc, gfin_ref[...])


def _combine(dest, gates, x1, gfin, y_rows):
    n_tok = x1.shape[0]
    td = min(TD_ROWS, n_tok)
    n_steps = n_tok // td
    dest3 = dest.reshape(n_steps, 1, td * TOP_K)
    return pl.pallas_call(
        _combine_body,
        grid=(n_steps,),
        in_specs=[
            pl.BlockSpec((1, 1, td * TOP_K), lambda i: (i, 0, 0), memory_space=pltpu.SMEM),
            pl.BlockSpec((1, 1, td * TOP_K), lambda i: (jnp.minimum(i + 1, n_steps - 1), 0, 0),
                         memory_space=pltpu.SMEM),
            pl.BlockSpec((td, 2 * TOP_K), lambda i: (i, 0)),
            pl.BlockSpec((td, D_MODEL), lambda i: (i, 0)),
            pl.BlockSpec((1, D_MODEL), lambda i: (0, 0)),
            pl.BlockSpec(memory_space=pl.ANY),
        ],
        out_specs=pl.BlockSpec((td, D_MODEL), lambda i: (i, 0)),
        out_shape=jax.ShapeDtypeStruct((n_tok, D_MODEL), F32),
        scratch_shapes=[pltpu.VMEM((2, TOP_K, td, SUBLANES, LANES), F32),
                        pltpu.SemaphoreType.DMA((2,))],
        compiler_params=pltpu.CompilerParams(
            dimension_semantics=("arbitrary",), vmem_limit_bytes=VMEM_LIMIT),
        name="combine",
    )(dest3, dest3, gates, x1, gfin, y_rows)


def _block_diag(p):
    g, a, b = p.shape
    eye = jnp.eye(g, dtype=p.dtype)
    return (p[:, :, None, :] * eye[:, None, :, None]).reshape(g * a, g * b)


def _retention_tables():
    log_g = jnp.log1p(-(2.0 ** (-5.0 - jnp.arange(RET_HEADS, dtype=F32))))
    pos = jnp.arange(RET_CHUNK, dtype=F32)
    rel = pos[:, None] - pos[None, :]
    intra = jnp.where(rel >= 0, jnp.exp(log_g[:, None, None] * jnp.maximum(rel, 0.0)), 0.0)
    q_decay = jnp.exp(log_g[:, None] * (pos + 1.0))
    k_decay = jnp.exp(log_g[:, None] * (RET_CHUNK - 1.0 - pos))
    chunk_decay = jnp.exp(log_g * RET_CHUNK)
    qd = jnp.broadcast_to(q_decay[:, :, None], (RET_HEADS, RET_CHUNK, RET_DK))
    kd = jnp.broadcast_to(k_decay[:, :, None], (RET_HEADS, RET_CHUNK, RET_DK))
    cd = jnp.broadcast_to(chunk_decay[:, None, None], (RET_HEADS, 1, RET_DV))
    return intra, qd, kd, cd


def _layer(x, positions, norm_mix_g, w_in, lam_re, lam_im, log_dt, b_re, b_im, c_re, c_im,
           s5_d, w_glu, b_glu, ret_gn_g, w_branch_s5, w_branch_ret, w_out, norm_ffn_g,
           router_w, router_b, w1, b1, w2, b2, g_out):
    bsz, seqlen, _ = x.shape
    n_tok = bsz * seqlen

    half = RET_DK // 2
    inv_freq = ROPE_BASE ** (-jnp.arange(half, dtype=F32) / half)
    rope = jnp.stack([jnp.concatenate([inv_freq, inv_freq]),
                      jnp.concatenate([-jnp.ones((half,), F32), jnp.ones((half,), F32)])])
    pos128 = jnp.broadcast_to(positions.astype(F32)[..., None], (bsz, seqlen, LANES))
    u, q, k, v, g_ret, gate_a, gate_b = _inproj(
        x, pos128, rope, norm_mix_g.reshape(1, D_MODEL), w_in.astype(BF16))

    lam = jnp.stack([lam_re.reshape(-1), lam_im.reshape(-1),
                     jnp.repeat(log_dt, S5_STATE)]).astype(F32)
    gpc = S5_GROUPS // S5_CHUNKS

    def chunked_block_diag(p):
        return jax.vmap(_block_diag)(p.reshape(S5_CHUNKS, gpc, p.shape[1], p.shape[2]))

    bblk = jnp.stack([chunked_block_diag(jnp.swapaxes(b_re, 1, 2)),
                      chunked_block_diag(jnp.swapaxes(b_im, 1, 2))])
    cblk = jnp.stack([chunked_block_diag(jnp.swapaxes(c_re, 1, 2)),
                      chunked_block_diag(jnp.swapaxes(c_im, 1, 2))]).astype(BF16)
    dvec = jnp.stack([s5_d, b_glu])
    ys5 = _s5(u, lam, bblk, cblk, dvec, w_glu.astype(BF16), bsz)

    intra, qd, kd, cd = _retention_tables()
    retg = _retention(q, k, v, g_ret, intra, qd, kd, cd, ret_gn_g.reshape(1, RET_V_WIDTH))

    rw_hi = router_w.astype(BF16)
    rw_lo = (router_w - rw_hi.astype(F32)).astype(BF16)
    rw = (jnp.zeros((D_MODEL, 2 * LANES), BF16).at[:, :N_EXPERTS].set(rw_hi)
          .at[:, LANES:LANES + N_EXPERTS].set(rw_lo))
    rb = jnp.zeros((1, LANES), F32).at[0, :N_EXPERTS].set(router_b)
    x1, h2, route, gates, counts = _merge(
        x, ys5, retg, gate_a, gate_b, w_branch_s5.astype(BF16), w_branch_ret.astype(BF16),
        w_out.astype(BF16), norm_ffn_g.reshape(1, D_MODEL), rw, rb)

    bm = BM_EXPERT
    n_assign = n_tok * TOP_K
    n_rows = (n_assign // bm + N_EXPERTS) * bm
    route = route.reshape(n_tok, 2 * TOP_K)
    top_e = route[:, :TOP_K]
    rank = route[:, TOP_K:2 * TOP_K]
    cnt = counts[0, :N_EXPERTS].astype(jnp.int32)
    padded = (cnt + bm - 1) // bm * bm
    pend = jnp.cumsum(padded)
    pstart = pend - padded
    dest = (pstart[top_e] + rank).astype(jnp.int32)
    block_start = jnp.arange(n_rows // bm, dtype=jnp.int32) * bm
    block_expert = jnp.minimum(
        jnp.sum((pend[None, :] <= block_start[:, None]).astype(jnp.int32), axis=1), N_EXPERTS - 1)
    n_active = (pend[N_EXPERTS - 1:] // bm).astype(jnp.int32)
    zinfo = jnp.concatenate([pstart + cnt, padded - cnt, n_active]).astype(jnp.int32)

    x_rows, w1p, w2b = _dispatch(zinfo, dest, h2.reshape(n_tok, SUBLANES, LANES), w1, w2,
                                 n_rows, bm)
    b1p = jnp.concatenate([b1[:, 0::2], b1[:, 1::2]], axis=-1).reshape(N_EXPERTS, 1, 2 * D_FF)
    y_rows = _experts(block_expert, n_active, x_rows, w1p, b1p, w2b,
                      b2.reshape(N_EXPERTS, 1, D_MODEL), bm)
    out = _combine(dest, gates.reshape(n_tok, 2 * TOP_K), x1.reshape(n_tok, D_MODEL),
                   g_out.reshape(1, D_MODEL), y_rows)
    return out.reshape(bsz, seqlen, D_MODEL)


def kernel(x, positions, norm_mix_g, w_in, s5_lambda_re, s5_lambda_im, s5_log_dt, s5_b_re, s5_b_im, s5_c_re, s5_c_im, s5_d, s5_w_glu, s5_b_glu, ret_gn_g, w_branch_s5, w_branch_ret, w_out, norm_ffn_g, router_w, router_b, expert_w1, expert_b1, expert_w2, expert_b2, norm_final_g):
    assert norm_mix_g.shape[0] == 1, "single-layer trunk"
    return _layer(x, positions, norm_mix_g[0], w_in[0], s5_lambda_re[0], s5_lambda_im[0],
                  s5_log_dt[0], s5_b_re[0], s5_b_im[0], s5_c_re[0], s5_c_im[0], s5_d[0],
                  s5_w_glu[0], s5_b_glu[0], ret_gn_g[0], w_branch_s5[0], w_branch_ret[0],
                  w_out[0], norm_ffn_g[0], router_w[0], router_b[0], expert_w1[0], expert_b1[0],
                  expert_w2[0], expert_b2[0], norm_final_g)
```

```python
import jax
import jax.numpy as jnp
from jax import lax
from jax.experimental import pallas as pl
from jax.experimental.pallas import tpu as pltpu

F32 = jnp.float32
BF16 = jnp.bfloat16

D_MODEL = 1024
S5_WIDTH = 512
S5_GROUP = 16
S5_GROUPS = 32
S5_STATE = 64
S5_LANES = S5_GROUPS * S5_STATE
RET_HEADS = 4
RET_DK = 128
RET_DV = 256
RET_QK_WIDTH = RET_HEADS * RET_DK
RET_V_WIDTH = RET_HEADS * RET_DV
RET_CHUNK = 128
ROPE_BASE = 10000.0
N_EXPERTS = 32
TOP_K = 4
D_FF = 1024
SWIGLU_ALPHA = 1.702
SWIGLU_LIMIT = 7.0
NORM_EPS = 1e-5
IN_WIDTH = S5_WIDTH + 2 * RET_QK_WIDTH + 2 * RET_V_WIDTH + 2 * D_MODEL
OFF_Q = S5_WIDTH
OFF_V = OFF_Q + 2 * RET_QK_WIDTH
OFF_G = OFF_V + RET_V_WIDTH
OFF_GA = OFF_G + RET_V_WIDTH
OFF_GB = OFF_GA + D_MODEL

LANES = 128
SUBLANES = 8
VMEM_LIMIT = 56 * 1024 * 1024

TM_PROJ = 512
S5_LC = 512
S5_CHUNKS = 4
TR_RET = 512
BM_EXPERT = 512
TD_ROWS = 512


def _rms(x, g):
    return x * lax.rsqrt(jnp.mean(x * x, axis=-1, keepdims=True) + NORM_EPS) * g


def _sigmoid(x):
    return 0.5 * jnp.tanh(0.5 * x) + 0.5


def _inproj_body(x_ref, pos_ref, rope_ref, g_ref, w_ref,
                 u_ref, q_ref, k_ref, v_ref, gr_ref, ga_ref, gb_ref):
    hb = _rms(x_ref[0], g_ref[...]).astype(BF16)

    def proj(lo, width):
        return jnp.dot(hb, w_ref[:, lo:lo + width], preferred_element_type=F32)

    u_ref[...] = proj(0, S5_WIDTH).astype(BF16)
    ang = pos_ref[0] * rope_ref[0:1, :]
    cos = jnp.cos(ang)
    sin = jnp.sin(ang) * rope_ref[1:2, :]
    qk = proj(OFF_Q, 2 * RET_QK_WIDTH)
    for h in range(RET_HEADS):
        qh = qk[:, h * RET_DK:(h + 1) * RET_DK]
        q_ref[0, :, h * RET_DK:(h + 1) * RET_DK] = (
            qh * cos + pltpu.roll(qh, RET_DK // 2, 1) * sin).astype(BF16)
        kh = qk[:, RET_QK_WIDTH + h * RET_DK:RET_QK_WIDTH + (h + 1) * RET_DK]
        k_ref[0, :, h * RET_DK:(h + 1) * RET_DK] = (
            (kh * cos + pltpu.roll(kh, RET_DK // 2, 1) * sin) * (RET_DK ** -0.5)).astype(BF16)
    v_ref[0] = proj(OFF_V, RET_V_WIDTH).astype(BF16)
    gr_ref[0] = proj(OFF_G, RET_V_WIDTH).astype(BF16)
    ga_ref[0] = proj(OFF_GA, D_MODEL).astype(BF16)
    gb_ref[0] = proj(OFF_GB, D_MODEL).astype(BF16)


def _inproj(x, pos128, rope, g, w_bf):
    bsz, seqlen, _ = x.shape
    tm = min(TM_PROJ, seqlen)
    grid = (bsz, seqlen // tm)
    row = lambda b, t: (b, t, 0)
    const = lambda b, t: (0, 0)
    out_shape = (
        jax.ShapeDtypeStruct((seqlen, bsz * S5_WIDTH), BF16),
        jax.ShapeDtypeStruct((bsz, seqlen, RET_QK_WIDTH), BF16),
        jax.ShapeDtypeStruct((bsz, seqlen, RET_QK_WIDTH), BF16),
        jax.ShapeDtypeStruct((bsz, seqlen, RET_V_WIDTH), BF16),
        jax.ShapeDtypeStruct((bsz, seqlen, RET_V_WIDTH), BF16),
        jax.ShapeDtypeStruct((bsz, seqlen, D_MODEL), BF16),
        jax.ShapeDtypeStruct((bsz, seqlen, D_MODEL), BF16),
    )
    return pl.pallas_call(
        _inproj_body,
        grid=grid,
        in_specs=[
            pl.BlockSpec((1, tm, D_MODEL), row),
            pl.BlockSpec((1, tm, LANES), row),
            pl.BlockSpec((2, LANES), const),
            pl.BlockSpec((1, D_MODEL), const),
            pl.BlockSpec((D_MODEL, IN_WIDTH), const, pipeline_mode=pl.Buffered(1)),
        ],
        out_specs=(
            pl.BlockSpec((tm, S5_WIDTH), lambda b, t: (t, b)),
            pl.BlockSpec((1, tm, RET_QK_WIDTH), row),
            pl.BlockSpec((1, tm, RET_QK_WIDTH), row),
            pl.BlockSpec((1, tm, RET_V_WIDTH), row),
            pl.BlockSpec((1, tm, RET_V_WIDTH), row),
            pl.BlockSpec((1, tm, D_MODEL), row),
            pl.BlockSpec((1, tm, D_MODEL), row),
        ),
        out_shape=out_shape,
        compiler_params=pltpu.CompilerParams(
            dimension_semantics=("arbitrary", "arbitrary"), vmem_limit_bytes=VMEM_LIMIT),
        name="inproj",
    )(x, pos128, rope, g, w_bf)


def _s5_body(u_ref, lam_ref, bblk_ref, cblk_ref, dvec_ref, wglu_ref, w1_ref, perm_ref, w2_ref,
             o_ref, w1o_ref, w2o_ref, bf_ref, a8_ref, p_ref, xre_ref, xim_ref):
    for c in range(D_FF // LANES):
        wc = w1_ref[:, c * 2 * LANES:(c + 1) * 2 * LANES].astype(BF16)
        r = jnp.dot(wc, perm_ref[...], preferred_element_type=F32).astype(BF16)
        w1o_ref[:, c * LANES:(c + 1) * LANES] = r[:, :LANES]
        w1o_ref[:, D_FF + c * LANES:D_FF + (c + 1) * LANES] = r[:, LANES:]
    w2o_ref[...] = w2_ref[...].astype(BF16)

    tt = u_ref.shape[0]
    rows = xre_ref.shape[0]
    bsz = rows // tt
    cw = S5_WIDTH // S5_CHUNKS
    cl = S5_LANES // S5_CHUNKS

    @pl.when(pl.program_id(0) == 0)
    def _init():
        lr = lam_ref[0:1, :]
        li = lam_ref[1:2, :]
        dt = jnp.exp(lam_ref[2:3, :])
        mag = jnp.exp(lr * dt)
        a_re = mag * jnp.cos(li * dt)
        a_im = mag * jnp.sin(li * dt)
        nr = a_re - 1.0
        den = lr * lr + li * li
        f_re = (nr * lr + a_im * li) / den
        f_im = (a_im * lr - nr * li) / den
        for c in range(S5_CHUNKS):
            fr = f_re[:, c * cl:(c + 1) * cl]
            fi = f_im[:, c * cl:(c + 1) * cl]
            b_re = bblk_ref[0, c]
            b_im = bblk_ref[1, c]
            bf_ref[0, c] = (b_re * fr - b_im * fi).astype(BF16)
            bf_ref[1, c] = (b_im * fr + b_re * fi).astype(BF16)
        a8_ref[0] = jnp.broadcast_to(a_re, (SUBLANES, S5_LANES))
        a8_ref[1] = jnp.broadcast_to(a_im, (SUBLANES, S5_LANES))
        p_ref[...] = jnp.zeros_like(p_ref)

    r_i = lax.broadcasted_iota(jnp.int32, (rows, tt), 0)
    t_i = lax.broadcasted_iota(jnp.int32, (rows, tt), 1)
    u2 = jnp.zeros((rows, S5_WIDTH), F32)
    for bb in range(bsz):
        spread = (r_i == bsz * t_i + bb).astype(BF16)
        u2 = u2 + jnp.dot(spread, u_ref[:, bb * S5_WIDTH:(bb + 1) * S5_WIDTH],
                          preferred_element_type=F32)
    ub = u2.astype(BF16)
    for c in range(S5_CHUNKS):
        uc = ub[:, c * cw:(c + 1) * cw]
        xre_ref[:, c * cl:(c + 1) * cl] = jnp.dot(uc, bf_ref[0, c], preferred_element_type=F32)
        xim_ref[:, c * cl:(c + 1) * cl] = jnp.dot(uc, bf_ref[1, c], preferred_element_type=F32)

    lower = lax.broadcasted_iota(jnp.int32, (SUBLANES, S5_LC), 0) < (SUBLANES // 2)
    for j in range(S5_LANES // S5_LC):
        sl = slice(j * S5_LC, (j + 1) * S5_LC)
        ar = a8_ref[0, :, sl]
        ai = a8_ref[1, :, sl]

        def step(k, carry, sl=sl, ar=ar, ai=ai):
            pr, pi = carry
            r0 = pl.multiple_of(k * SUBLANES, SUBLANES)
            xr = xre_ref[pl.ds(r0, SUBLANES), sl]
            xi = xim_ref[pl.ds(r0, SUBLANES), sl]
            s1r = ar * pr - ai * pi + xr
            s1i = ar * pi + ai * pr + xi
            tr = pltpu.roll(s1r, SUBLANES // 2, 0)
            ti = pltpu.roll(s1i, SUBLANES // 2, 0)
            s2r = ar * tr - ai * ti + xr
            s2i = ar * ti + ai * tr + xi
            xre_ref[pl.ds(r0, SUBLANES), sl] = jnp.where(lower, s1r, s2r)
            xim_ref[pl.ds(r0, SUBLANES), sl] = jnp.where(lower, s1i, s2i)
            return pltpu.roll(s2r, SUBLANES // 2, 0), pltpu.roll(s2i, SUBLANES // 2, 0)

        pr, pi = lax.fori_loop(0, rows // SUBLANES, step, (p_ref[0, :, sl], p_ref[1, :, sl]),
                               unroll=2)
        p_ref[0, :, sl] = pr
        p_ref[1, :, sl] = pi

    y = jnp.concatenate([
        jnp.dot(xre_ref[:, c * cl:(c + 1) * cl].astype(BF16), cblk_ref[0, c],
                preferred_element_type=F32)
        - jnp.dot(xim_ref[:, c * cl:(c + 1) * cl].astype(BF16), cblk_ref[1, c],
                  preferred_element_type=F32)
        for c in range(S5_CHUNKS)], axis=1)
    y = y + dvec_ref[0:1, :] * u2
    z = jax.nn.gelu(y)
    gate = jax.nn.sigmoid(
        jnp.dot(z.astype(BF16), wglu_ref[...], preferred_element_type=F32) + dvec_ref[1:2, :])
    out = (z * gate).astype(BF16)
    t_o = lax.broadcasted_iota(jnp.int32, (tt, rows), 0)
    r_o = lax.broadcasted_iota(jnp.int32, (tt, rows), 1)
    for bb in range(bsz):
        pick = (r_o == bsz * t_o + bb).astype(BF16)
        o_ref[:, bb * S5_WIDTH:(bb + 1) * S5_WIDTH] = jnp.dot(
            pick, out, preferred_element_type=F32).astype(BF16)


def _s5(u, lam, bblk, cblk, dvec, wglu, bsz, w1, w2):
    seqlen = u.shape[0]
    n_steps = 2 * N_EXPERTS
    tt = seqlen // n_steps
    rows = tt * bsz
    assert tt * n_steps == seqlen and rows % (2 * SUBLANES) == 0 and SUBLANES % bsz == 0
    kr = N_EXPERTS * D_MODEL // n_steps
    const2 = lambda i: (0, 0)
    const4 = lambda i: (0, 0, 0, 0)
    cw = S5_WIDTH // S5_CHUNKS
    cl = S5_LANES // S5_CHUNKS
    i_p = lax.broadcasted_iota(jnp.int32, (2 * LANES, 2 * LANES), 0)
    j_p = lax.broadcasted_iota(jnp.int32, (2 * LANES, 2 * LANES), 1)
    perm = (i_p == jnp.where(j_p < LANES, 2 * j_p, 2 * (j_p - LANES) + 1)).astype(BF16)
    y, w1p, w2b = pl.pallas_call(
        _s5_body,
        grid=(n_steps,),
        in_specs=[
            pl.BlockSpec((tt, bsz * S5_WIDTH), lambda i: (i, 0)),
            pl.BlockSpec((3, S5_LANES), const2),
            pl.BlockSpec((2, S5_CHUNKS, cw, cl), const4),
            pl.BlockSpec((2, S5_CHUNKS, cl, cw), const4),
            pl.BlockSpec((2, S5_WIDTH), const2),
            pl.BlockSpec((S5_WIDTH, S5_WIDTH), const2),
            pl.BlockSpec((kr, 2 * D_FF), lambda i: (i, 0)),
            pl.BlockSpec((2 * LANES, 2 * LANES), const2),
            pl.BlockSpec((kr, D_MODEL), lambda i: (i, 0)),
        ],
        out_specs=(
            pl.BlockSpec((tt, bsz * S5_WIDTH), lambda i: (i, 0)),
            pl.BlockSpec((kr, 2 * D_FF), lambda i: (i, 0)),
            pl.BlockSpec((kr, D_MODEL), lambda i: (i, 0)),
        ),
        out_shape=(
            jax.ShapeDtypeStruct((seqlen, bsz * S5_WIDTH), BF16),
            jax.ShapeDtypeStruct((N_EXPERTS * D_MODEL, 2 * D_FF), BF16),
            jax.ShapeDtypeStruct((N_EXPERTS * D_FF, D_MODEL), BF16),
        ),
        scratch_shapes=[
            pltpu.VMEM((2, S5_CHUNKS, cw, cl), BF16),
            pltpu.VMEM((2, SUBLANES, S5_LANES), F32),
            pltpu.VMEM((2, SUBLANES, S5_LANES), F32),
            pltpu.VMEM((rows, S5_LANES), F32),
            pltpu.VMEM((rows, S5_LANES), F32),
        ],
        compiler_params=pltpu.CompilerParams(
            dimension_semantics=("arbitrary",), vmem_limit_bytes=VMEM_LIMIT),
        name="s5_mixer",
    )(u, lam, bblk, cblk, dvec, wglu, w1.reshape(N_EXPERTS * D_MODEL, 2 * D_FF), perm,
      w2.reshape(N_EXPERTS * D_FF, D_MODEL))
    return (y, w1p.reshape(N_EXPERTS, D_MODEL, 2 * D_FF), w2b.reshape(N_EXPERTS, D_FF, D_MODEL))


def _ret_body(q_ref, k_ref, v_ref, g_ref, intra_ref, qd_ref, kd_ref, cd_ref, gn_ref, o_ref,
              st_ref):
    tr = q_ref.shape[1]

    @pl.when(pl.program_id(1) == 0)
    def _init():
        st_ref[...] = jnp.zeros_like(st_ref)

    for h in range(RET_HEADS):
        qs = slice(h * RET_DK, (h + 1) * RET_DK)
        vs = slice(h * RET_DV, (h + 1) * RET_DV)
        for c in range(tr // RET_CHUNK):
            rs = slice(c * RET_CHUNK, (c + 1) * RET_CHUNK)
            qc = q_ref[0, rs, qs]
            kc = k_ref[0, rs, qs]
            vc = v_ref[0, rs, vs]
            scores = lax.dot_general(qc, kc, (((1,), (1,)), ((), ())),
                                     preferred_element_type=F32) * intra_ref[h]
            inner = jnp.dot(scores.astype(BF16), vc, preferred_element_type=F32)
            st = st_ref[h]
            qdec = (qc.astype(F32) * qd_ref[h]).astype(BF16)
            cross = jnp.dot(qdec, st.astype(BF16), preferred_element_type=F32)
            kdec_t = (kc.astype(F32) * kd_ref[h]).T.astype(BF16)
            st_ref[h] = cd_ref[h] * st + jnp.dot(kdec_t, vc, preferred_element_type=F32)
            ret = inner + cross
            mu = jnp.mean(ret, axis=-1, keepdims=True)
            dev = ret - mu
            var = jnp.mean(dev * dev, axis=-1, keepdims=True)
            yn = dev * lax.rsqrt(var + NORM_EPS) * gn_ref[0:1, vs]
            o_ref[0, rs, vs] = (jax.nn.silu(g_ref[0, rs, vs].astype(F32)) * yn).astype(BF16)


def _retention(q, k, v, g, intra, qd, kd, cd, gn):
    bsz, seqlen, _ = q.shape
    tr = min(TR_RET, seqlen)
    row = lambda b, t: (b, t, 0)
    const3 = lambda b, t: (0, 0, 0)
    return pl.pallas_call(
        _ret_body,
        grid=(bsz, seqlen // tr),
        in_specs=[
            pl.BlockSpec((1, tr, RET_QK_WIDTH), row),
            pl.BlockSpec((1, tr, RET_QK_WIDTH), row),
            pl.BlockSpec((1, tr, RET_V_WIDTH), row),
            pl.BlockSpec((1, tr, RET_V_WIDTH), row),
            pl.BlockSpec((RET_HEADS, RET_CHUNK, RET_CHUNK), const3),
            pl.BlockSpec((RET_HEADS, RET_CHUNK, RET_DK), const3),
            pl.BlockSpec((RET_HEADS, RET_CHUNK, RET_DK), const3),
            pl.BlockSpec((RET_HEADS, 1, RET_DV), const3),
            pl.BlockSpec((1, RET_V_WIDTH), lambda b, t: (0, 0)),
        ],
        out_specs=pl.BlockSpec((1, tr, RET_V_WIDTH), row),
        out_shape=jax.ShapeDtypeStruct((bsz, seqlen, RET_V_WIDTH), BF16),
        scratch_shapes=[pltpu.VMEM((RET_HEADS, RET_DK, RET_DV), F32)],
        compiler_params=pltpu.CompilerParams(
            dimension_semantics=("arbitrary", "arbitrary"), vmem_limit_bytes=VMEM_LIMIT),
        name="retention",
    )(q, k, v, g, intra, qd, kd, cd, gn)


def _merge_body(x_ref, ys_ref, rg_ref, ga_ref, gb_ref, wbs_ref, wbr_ref, wo_ref, gf_ref,
                rw_ref, rb_ref,
                x1_ref, ri_ref, gt_ref, cnt_ref, xrows_ref,
                run_ref, hrows_ref, dvm_ref, dsm_ref, sems, isem):
    tm = x_ref.shape[1]
    cap = xrows_ref.shape[0] // N_EXPERTS
    step = pl.program_id(0) * pl.num_programs(1) + pl.program_id(1)
    last = pl.num_programs(0) * pl.num_programs(1) - 1

    @pl.when(step == 0)
    def _init():
        run_ref[...] = jnp.zeros_like(run_ref)

    y_a = jnp.dot(ys_ref[...], wbs_ref[...], preferred_element_type=F32)
    y_b = jnp.dot(rg_ref[0], wbr_ref[...], preferred_element_type=F32)
    merged = _sigmoid(ga_ref[0].astype(F32)) * y_a + _sigmoid(gb_ref[0].astype(F32)) * y_b
    x1 = x_ref[0] + jnp.dot(merged.astype(BF16), wo_ref[...], preferred_element_type=F32)
    x1_ref[0] = x1
    h2 = _rms(x1, gf_ref[...])

    lane = lax.broadcasted_iota(jnp.int32, (tm, LANES), 1)
    h_hi = h2.astype(BF16)
    h_lo = (h2 - h_hi.astype(F32)).astype(BF16)
    hi_terms = jnp.dot(h_hi, rw_ref[...], preferred_element_type=F32)
    logits = (hi_terms[:, :LANES] + hi_terms[:, LANES:]
              + jnp.dot(h_lo, rw_ref[:, :LANES], preferred_element_type=F32)) + rb_ref[...]
    work = jnp.where(lane < N_EXPERTS, logits, -jnp.inf)
    lane_f = lane.astype(F32)
    vals, hots, idxs = [], [], []
    for kk in range(TOP_K):
        m = jnp.max(work, axis=-1, keepdims=True)
        idx = jnp.min(jnp.where(work == m, lane_f, float(LANES)), axis=-1, keepdims=True)
        hot = lane_f == idx
        vals.append(m)
        hots.append(hot)
        idxs.append(idx)
        work = jnp.where(hot, -jnp.inf, work)
    exps = [jnp.exp(v - vals[0]) for v in vals]
    denom = exps[0] + exps[1] + exps[2] + exps[3]
    gt = jnp.zeros((tm, LANES), F32)
    for kk in range(TOP_K):
        gt = jnp.where(lane == kk, exps[kk] / denom, gt)
    gt_ref[0] = gt[:, :2 * TOP_K]

    onehot = [h.astype(F32) for h in hots]
    tot = onehot[0] + onehot[1] + onehot[2] + onehot[3]
    r_i = lax.broadcasted_iota(jnp.int32, (tm, tm), 0)
    c_i = lax.broadcasted_iota(jnp.int32, (tm, tm), 1)
    tril = (c_i < r_i).astype(BF16)
    base = jnp.dot(tril, tot.astype(BF16), preferred_element_type=F32) + run_ref[...]
    dmat = jnp.zeros((tm, LANES), F32)
    for kk in range(TOP_K):
        rank = jnp.sum(onehot[kk] * base, axis=-1, keepdims=True)
        dmat = jnp.where(lane == kk, idxs[kk] * float(cap) + rank, dmat)
    ri_ref[0] = dmat.astype(jnp.int32)[:, :2 * TOP_K]
    run_ref[...] = run_ref[...] + jnp.sum(tot, axis=0, keepdims=True)
    cnt_ref[...] = run_ref[...]

    dvm_ref[...] = dmat.T[:SUBLANES, :].astype(jnp.int32)
    to_smem = pltpu.make_async_copy(dvm_ref, dsm_ref, isem)
    to_smem.start()

    def drain(which):
        def wait(r, c):
            for kk in range(TOP_K):
                pltpu.make_async_copy(hrows_ref.at[which, 0], xrows_ref.at[0],
                                      sems.at[which]).wait()
            return c
        lax.fori_loop(0, tm, wait, 0, unroll=8)

    for par in range(2):
        @pl.when(step % 2 == par)
        def _scatter(par=par):
            hrows_ref[par] = h2.reshape(tm, SUBLANES, LANES)
            to_smem.wait()

            def start(r, c):
                for kk in range(TOP_K):
                    pltpu.make_async_copy(hrows_ref.at[par, r], xrows_ref.at[dsm_ref[kk, r]],
                                          sems.at[par]).start(priority=kk % 2)
                return c

            lax.fori_loop(0, tm, start, 0, unroll=8)

            @pl.when(step > 0)
            def _wait_previous():
                drain(1 - par)

            @pl.when(step == last)
            def _wait_last():
                drain(par)


def _merge(x, ys5, retg, ga, gb, wbs, wbr, wo, gf, rw, rb):
    bsz, seqlen, _ = x.shape
    n_tok = bsz * seqlen
    tm = min(TM_PROJ, seqlen)
    cap = n_tok
    row = lambda b, t: (b, t, 0)
    const = lambda b, t: (0, 0)
    out_shape = (
        jax.ShapeDtypeStruct((bsz, seqlen, D_MODEL), F32),
        jax.ShapeDtypeStruct((bsz, seqlen, 2 * TOP_K), jnp.int32),
        jax.ShapeDtypeStruct((bsz, seqlen, 2 * TOP_K), F32),
        jax.ShapeDtypeStruct((1, LANES), F32),
        jax.ShapeDtypeStruct((N_EXPERTS * cap, SUBLANES, LANES), F32),
    )
    return pl.pallas_call(
        _merge_body,
        grid=(bsz, seqlen // tm),
        in_specs=[
            pl.BlockSpec((1, tm, D_MODEL), row),
            pl.BlockSpec((tm, S5_WIDTH), lambda b, t: (t, b)),
            pl.BlockSpec((1, tm, RET_V_WIDTH), row),
            pl.BlockSpec((1, tm, D_MODEL), row),
            pl.BlockSpec((1, tm, D_MODEL), row),
            pl.BlockSpec((S5_WIDTH, D_MODEL), const),
            pl.BlockSpec((RET_V_WIDTH, D_MODEL), const),
            pl.BlockSpec((D_MODEL, D_MODEL), const),
            pl.BlockSpec((1, D_MODEL), const),
            pl.BlockSpec((D_MODEL, 2 * LANES), const),
            pl.BlockSpec((1, LANES), const),
        ],
        out_specs=(
            pl.BlockSpec((1, tm, D_MODEL), row),
            pl.BlockSpec((1, tm, 2 * TOP_K), row),
            pl.BlockSpec((1, tm, 2 * TOP_K), row),
            pl.BlockSpec((1, LANES), const),
            pl.BlockSpec(memory_space=pl.ANY),
        ),
        out_shape=out_shape,
        scratch_shapes=[
            pltpu.VMEM((1, LANES), F32),
            pltpu.VMEM((2, tm, SUBLANES, LANES), F32),
            pltpu.VMEM((SUBLANES, tm), jnp.int32),
            pltpu.SMEM((SUBLANES, tm), jnp.int32),
            pltpu.SemaphoreType.DMA((2,)),
            pltpu.SemaphoreType.DMA(()),
        ],
        compiler_params=pltpu.CompilerParams(
            dimension_semantics=("arbitrary", "arbitrary"), has_side_effects=True,
            vmem_limit_bytes=VMEM_LIMIT),
        name="merge_router_dispatch",
    )(x, ys5, retg, ga, gb, wbs, wbr, wo, gf, rw, rb)


def _expert_body(be_ref, brow_ref, bvalid_ref, na_ref, x_ref, w1_ref, b1_ref, w2_ref, b2_ref,
                 y_ref):
    del be_ref, brow_ref
    i = pl.program_id(0)

    @pl.when(i < na_ref[0])
    def _compute():
        bm = x_ref.shape[0]
        row = lax.broadcasted_iota(jnp.int32, (bm, D_MODEL), 0)
        xb = jnp.where(row < bvalid_ref[i], x_ref[...].reshape(bm, D_MODEL), 0.0).astype(BF16)
        gu = jnp.dot(xb, w1_ref[0], preferred_element_type=F32) + b1_ref[0]
        x_glu = jnp.minimum(gu[:, :D_FF], SWIGLU_LIMIT)
        x_lin = jnp.clip(gu[:, D_FF:], -SWIGLU_LIMIT, SWIGLU_LIMIT)
        act = x_glu * jax.nn.sigmoid(SWIGLU_ALPHA * x_glu) * (x_lin + 1.0)
        y = jnp.dot(act.astype(BF16), w2_ref[0], preferred_element_type=F32) + b2_ref[0]
        y_ref[...] = y.reshape(bm, SUBLANES, LANES)


def _experts(block_expert, block_row, block_valid, n_active, x_rows, w1, b1, w2, b2, bm):
    n_blocks = block_expert.shape[0]
    xmap = lambda i, be, br, bv, na: (br[i], 0, 0)
    wmap = lambda i, be, br, bv, na: (be[i], 0, 0)
    grid_spec = pltpu.PrefetchScalarGridSpec(
        num_scalar_prefetch=4,
        grid=(n_blocks,),
        in_specs=[
            pl.BlockSpec((bm, SUBLANES, LANES), xmap),
            pl.BlockSpec((1, D_MODEL, 2 * D_FF), wmap),
            pl.BlockSpec((1, 1, 2 * D_FF), wmap),
            pl.BlockSpec((1, D_FF, D_MODEL), wmap),
            pl.BlockSpec((1, 1, D_MODEL), wmap),
        ],
        out_specs=pl.BlockSpec((bm, SUBLANES, LANES), xmap),
    )
    return pl.pallas_call(
        _expert_body,
        grid_spec=grid_spec,
        out_shape=jax.ShapeDtypeStruct(x_rows.shape, F32),
        compiler_params=pltpu.CompilerParams(
            dimension_semantics=("arbitrary",), vmem_limit_bytes=VMEM_LIMIT),
        name="expert_ffn",
    )(block_expert, block_row, block_valid, n_active, x_rows, w1, b1, w2, b2)


def _combine_body(dest_ref, nxt_ref, gt_ref, x1_ref, gfin_ref, yrows_ref, o_ref, ybuf_ref, sems):
    td = x1_ref.shape[0]
    step = pl.program_id(0)
    slot = step % 2

    def issue(idx_ref, which):
        def start(r, c):
            for kk in range(TOP_K):
                d = idx_ref[0, 0, r * TOP_K + kk]
                pltpu.make_async_copy(yrows_ref.at[d], ybuf_ref.at[which, kk, r],
                                      sems.at[which]).start(priority=kk % 2)
            return c
        lax.fori_loop(0, td, start, 0, unroll=8)

    @pl.when(step == 0)
    def _first():
        issue(dest_ref, 0)

    for par in range(2):
        @pl.when((step < pl.num_programs(0) - 1) & (slot == par))
        def _prefetch(par=par):
            issue(nxt_ref, 1 - par)

    def wait(r, c):
        for kk in range(TOP_K):
            pltpu.make_async_copy(yrows_ref.at[0], ybuf_ref.at[0, kk, 0], sems.at[slot]).wait()
        return c

    lax.fori_loop(0, td, wait, 0, unroll=8)
    gt = gt_ref[...]
    acc = x1_ref[...]
    for kk in range(TOP_K):
        acc = acc + gt[:, kk:kk + 1] * ybuf_ref[slot, kk].reshape(td, D_MODEL)
    o_ref[...] = _rms(acc, gfin_ref[...])


def _combine(dest, gates, x1, gfin, y_rows):
    n_tok = x1.shape[0]
    td = min(TD_ROWS, n_tok)
    n_steps = n_tok // td
    dest3 = dest.reshape(n_steps, 1, td * TOP_K)
    return pl.pallas_call(
        _combine_body,
        grid=(n_steps,),
        in_specs=[
            pl.BlockSpec((1, 1, td * TOP_K), lambda i: (i, 0, 0), memory_space=pltpu.SMEM),
            pl.BlockSpec((1, 1, td * TOP_K), lambda i: (jnp.minimum(i + 1, n_steps - 1), 0, 0),
                         memory_space=pltpu.SMEM),
            pl.BlockSpec((td, 2 * TOP_K), lambda i: (i, 0)),
            pl.BlockSpec((td, D_MODEL), lambda i: (i, 0)),
            pl.BlockSpec((1, D_MODEL), lambda i: (0, 0)),
            pl.BlockSpec(memory_space=pl.ANY),
        ],
        out_specs=pl.BlockSpec((td, D_MODEL), lambda i: (i, 0)),
        out_shape=jax.ShapeDtypeStruct((n_tok, D_MODEL), F32),
        scratch_shapes=[pltpu.VMEM((2, TOP_K, td, SUBLANES, LANES), F32),
                        pltpu.SemaphoreType.DMA((2,))],
        compiler_params=pltpu.CompilerParams(
            dimension_semantics=("arbitrary",), vmem_limit_bytes=VMEM_LIMIT),
        name="combine",
    )(dest3, dest3, gates, x1, gfin, y_rows)


def _block_diag(p):
    g, a, b = p.shape
    eye = jnp.eye(g, dtype=p.dtype)
    return (p[:, :, None, :] * eye[:, None, :, None]).reshape(g * a, g * b)


def _retention_tables():
    log_g = jnp.log1p(-(2.0 ** (-5.0 - jnp.arange(RET_HEADS, dtype=F32))))
    pos = jnp.arange(RET_CHUNK, dtype=F32)
    rel = pos[:, None] - pos[None, :]
    intra = jnp.where(rel >= 0, jnp.exp(log_g[:, None, None] * jnp.maximum(rel, 0.0)), 0.0)
    q_decay = jnp.exp(log_g[:, None] * (pos + 1.0))
    k_decay = jnp.exp(log_g[:, None] * (RET_CHUNK - 1.0 - pos))
    chunk_decay = jnp.exp(log_g * RET_CHUNK)
    qd = jnp.broadcast_to(q_decay[:, :, None], (RET_HEADS, RET_CHUNK, RET_DK))
    kd = jnp.broadcast_to(k_decay[:, :, None], (RET_HEADS, RET_CHUNK, RET_DK))
    cd = jnp.broadcast_to(chunk_decay[:, None, None], (RET_HEADS, 1, RET_DV))
    return intra, qd, kd, cd


def _layer(x, positions, norm_mix_g, w_in, lam_re, lam_im, log_dt, b_re, b_im, c_re, c_im,
           s5_d, w_glu, b_glu, ret_gn_g, w_branch_s5, w_branch_ret, w_out, norm_ffn_g,
           router_w, router_b, w1, b1, w2, b2, g_out):
    bsz, seqlen, _ = x.shape
    n_tok = bsz * seqlen

    half = RET_DK // 2
    inv_freq = ROPE_BASE ** (-jnp.arange(half, dtype=F32) / half)
    rope = jnp.stack([jnp.concatenate([inv_freq, inv_freq]),
                      jnp.concatenate([-jnp.ones((half,), F32), jnp.ones((half,), F32)])])
    pos128 = jnp.broadcast_to(positions.astype(F32)[..., None], (bsz, seqlen, LANES))
    u, q, k, v, g_ret, gate_a, gate_b = _inproj(
        x, pos128, rope, norm_mix_g.reshape(1, D_MODEL), w_in.astype(BF16))

    lam = jnp.stack([lam_re.reshape(-1), lam_im.reshape(-1),
                     jnp.repeat(log_dt, S5_STATE)]).astype(F32)
    gpc = S5_GROUPS // S5_CHUNKS

    def chunked_block_diag(p):
        return jax.vmap(_block_diag)(p.reshape(S5_CHUNKS, gpc, p.shape[1], p.shape[2]))

    bblk = jnp.stack([chunked_block_diag(jnp.swapaxes(b_re, 1, 2)),
                      chunked_block_diag(jnp.swapaxes(b_im, 1, 2))])
    cblk = jnp.stack([chunked_block_diag(jnp.swapaxes(c_re, 1, 2)),
                      chunked_block_diag(jnp.swapaxes(c_im, 1, 2))]).astype(BF16)
    dvec = jnp.stack([s5_d, b_glu])
    ys5, w1p, w2b = _s5(u, lam, bblk, cblk, dvec, w_glu.astype(BF16), bsz, w1, w2)

    intra, qd, kd, cd = _retention_tables()
    retg = _retention(q, k, v, g_ret, intra, qd, kd, cd, ret_gn_g.reshape(1, RET_V_WIDTH))

    rw_hi = router_w.astype(BF16)
    rw_lo = (router_w - rw_hi.astype(F32)).astype(BF16)
    rw = (jnp.zeros((D_MODEL, 2 * LANES), BF16).at[:, :N_EXPERTS].set(rw_hi)
          .at[:, LANES:LANES + N_EXPERTS].set(rw_lo))
    rb = jnp.zeros((1, LANES), F32).at[0, :N_EXPERTS].set(router_b)
    x1, route, gates, counts, x_rows = _merge(
        x, ys5, retg, gate_a, gate_b, w_branch_s5.astype(BF16), w_branch_ret.astype(BF16),
        w_out.astype(BF16), norm_ffn_g.reshape(1, D_MODEL), rw, rb)

    bm = BM_EXPERT
    cap = n_tok
    n_blocks = n_tok * TOP_K // bm + N_EXPERTS
    dest = route.reshape(n_tok, 2 * TOP_K)[:, :TOP_K]
    cnt = counts[0, :N_EXPERTS].astype(jnp.int32)
    nblk = (cnt + bm - 1) // bm
    bend = jnp.cumsum(nblk)
    n_active = bend[N_EXPERTS - 1:]
    blk = jnp.minimum(jnp.arange(n_blocks, dtype=jnp.int32), n_active[0] - 1)
    block_expert = jnp.sum((bend[None, :] <= blk[:, None]).astype(jnp.int32), axis=1)
    sel = (jnp.arange(N_EXPERTS, dtype=jnp.int32)[None, :] == block_expert[:, None]).astype(jnp.int32)
    j_in = blk - jnp.sum(sel * (bend - nblk)[None, :], axis=1)
    block_row = block_expert * (cap // bm) + j_in
    block_valid = jnp.clip(jnp.sum(sel * cnt[None, :], axis=1) - j_in * bm, 0, bm)

    b1p = jnp.concatenate([b1[:, 0::2], b1[:, 1::2]], axis=-1).reshape(N_EXPERTS, 1, 2 * D_FF)
    y_rows = _experts(block_expert.astype(jnp.int32), block_row.astype(jnp.int32),
                      block_valid.astype(jnp.int32), n_active.astype(jnp.int32), x_rows, w1p, b1p,
                      w2b, b2.reshape(N_EXPERTS, 1, D_MODEL), bm)
    out = _combine(dest, gates.reshape(n_tok, 2 * TOP_K), x1.reshape(n_tok, D_MODEL),
                   g_out.reshape(1, D_MODEL), y_rows)
    return out.reshape(bsz, seqlen, D_MODEL)


def kernel(x, positions, norm_mix_g, w_in, s5_lambda_re, s5_lambda_im, s5_log_dt, s5_b_re, s5_b_im, s5_c_re, s5_c_im, s5_d, s5_w_glu, s5_b_glu, ret_gn_g, w_branch_s5, w_branch_ret, w_out, norm_ffn_g, router_w, router_b, expert_w1, expert_b1, expert_w2, expert_b2, norm_final_g):
    assert norm_mix_g.shape[0] == 1, "single-layer trunk"
    return _layer(x, positions, norm_mix_g[0], w_in[0], s5_lambda_re[0], s5_lambda_im[0],
                  s5_log_dt[0], s5_b_re[0], s5_b_im[0], s5_c_re[0], s5_c_im[0], s5_d[0],
                  s5_w_glu[0], s5_b_glu[0], ret_gn_g[0], w_branch_s5[0], w_branch_ret[0],
                  w_out[0], norm_ffn_g[0], router_w[0], router_b[0], expert_w1[0], expert_b1[0],
                  expert_w2[0], expert_b2[0], norm_final_g)
```

```python
import jax
import jax.numpy as jnp
from jax import lax
from jax.experimental import pallas as pl
from jax.experimental.pallas import tpu as pltpu

F32 = jnp.float32
BF16 = jnp.bfloat16

D_MODEL = 1024
S5_WIDTH = 512
S5_GROUP = 16
S5_GROUPS = 32
S5_STATE = 64
S5_LANES = S5_GROUPS * S5_STATE
RET_HEADS = 4
RET_DK = 128
RET_DV = 256
RET_QK_WIDTH = RET_HEADS * RET_DK
RET_V_WIDTH = RET_HEADS * RET_DV
RET_CHUNK = 128
ROPE_BASE = 10000.0
N_EXPERTS = 32
TOP_K = 4
D_FF = 1024
SWIGLU_ALPHA = 1.702
SWIGLU_LIMIT = 7.0
NORM_EPS = 1e-5
IN_WIDTH = S5_WIDTH + 2 * RET_QK_WIDTH + 2 * RET_V_WIDTH + 2 * D_MODEL
OFF_Q = S5_WIDTH
OFF_V = OFF_Q + 2 * RET_QK_WIDTH
OFF_G = OFF_V + RET_V_WIDTH
OFF_GA = OFF_G + RET_V_WIDTH
OFF_GB = OFF_GA + D_MODEL

LANES = 128
SUBLANES = 8
VMEM_LIMIT = 56 * 1024 * 1024

TM_PROJ = 512
S5_LC = 512
S5_CHUNKS = 4
TR_RET = 512
BM_EXPERT = 512
TD_ROWS = 512


def _rms(x, g):
    return x * lax.rsqrt(jnp.mean(x * x, axis=-1, keepdims=True) + NORM_EPS) * g


def _sigmoid(x):
    return 0.5 * jnp.tanh(0.5 * x) + 0.5


def _inproj_body(x_ref, pos_ref, rope_ref, g_ref, w_ref,
                 u_ref, q_ref, k_ref, v_ref, gr_ref, ga_ref, gb_ref):
    hb = _rms(x_ref[0], g_ref[...]).astype(BF16)

    def proj(lo, width):
        return jnp.dot(hb, w_ref[:, lo:lo + width], preferred_element_type=F32)

    u_ref[...] = proj(0, S5_WIDTH).astype(BF16)
    ang = pos_ref[0] * rope_ref[0:1, :]
    cos = jnp.cos(ang)
    sin = jnp.sin(ang) * rope_ref[1:2, :]
    qk = proj(OFF_Q, 2 * RET_QK_WIDTH)
    for h in range(RET_HEADS):
        qh = qk[:, h * RET_DK:(h + 1) * RET_DK]
        q_ref[0, :, h * RET_DK:(h + 1) * RET_DK] = (
            qh * cos + pltpu.roll(qh, RET_DK // 2, 1) * sin).astype(BF16)
        kh = qk[:, RET_QK_WIDTH + h * RET_DK:RET_QK_WIDTH + (h + 1) * RET_DK]
        k_ref[0, :, h * RET_DK:(h + 1) * RET_DK] = (
            (kh * cos + pltpu.roll(kh, RET_DK // 2, 1) * sin) * (RET_DK ** -0.5)).astype(BF16)
    v_ref[0] = proj(OFF_V, RET_V_WIDTH).astype(BF16)
    gr_ref[0] = proj(OFF_G, RET_V_WIDTH).astype(BF16)
    ga_ref[0] = proj(OFF_GA, D_MODEL).astype(BF16)
    gb_ref[0] = proj(OFF_GB, D_MODEL).astype(BF16)


def _inproj(x, pos128, rope, g, w_bf):
    bsz, seqlen, _ = x.shape
    tm = min(TM_PROJ, seqlen)
    grid = (bsz, seqlen // tm)
    row = lambda b, t: (b, t, 0)
    const = lambda b, t: (0, 0)
    out_shape = (
        jax.ShapeDtypeStruct((seqlen, bsz * S5_WIDTH), BF16),
        jax.ShapeDtypeStruct((bsz, seqlen, RET_QK_WIDTH), BF16),
        jax.ShapeDtypeStruct((bsz, seqlen, RET_QK_WIDTH), BF16),
        jax.ShapeDtypeStruct((bsz, seqlen, RET_V_WIDTH), BF16),
        jax.ShapeDtypeStruct((bsz, seqlen, RET_V_WIDTH), BF16),
        jax.ShapeDtypeStruct((bsz, seqlen, D_MODEL), BF16),
        jax.ShapeDtypeStruct((bsz, seqlen, D_MODEL), BF16),
    )
    return pl.pallas_call(
        _inproj_body,
        grid=grid,
        in_specs=[
            pl.BlockSpec((1, tm, D_MODEL), row),
            pl.BlockSpec((1, tm, LANES), row),
            pl.BlockSpec((2, LANES), const),
            pl.BlockSpec((1, D_MODEL), const),
            pl.BlockSpec((D_MODEL, IN_WIDTH), const, pipeline_mode=pl.Buffered(1)),
        ],
        out_specs=(
            pl.BlockSpec((tm, S5_WIDTH), lambda b, t: (t, b)),
            pl.BlockSpec((1, tm, RET_QK_WIDTH), row),
            pl.BlockSpec((1, tm, RET_QK_WIDTH), row),
            pl.BlockSpec((1, tm, RET_V_WIDTH), row),
            pl.BlockSpec((1, tm, RET_V_WIDTH), row),
            pl.BlockSpec((1, tm, D_MODEL), row),
            pl.BlockSpec((1, tm, D_MODEL), row),
        ),
        out_shape=out_shape,
        compiler_params=pltpu.CompilerParams(
            dimension_semantics=("arbitrary", "arbitrary"), vmem_limit_bytes=VMEM_LIMIT),
        name="inproj",
    )(x, pos128, rope, g, w_bf)


def _s5_body(u_ref, lam_ref, bblk_ref, cblk_ref, dvec_ref, wglu_ref, w1_ref, perm_ref, w2_ref,
             o_ref, w1o_ref, w2o_ref, bf_ref, a8_ref, p_ref, xre_ref, xim_ref):
    for c in range(D_FF // LANES):
        wc = w1_ref[:, c * 2 * LANES:(c + 1) * 2 * LANES].astype(BF16)
        r = jnp.dot(wc, perm_ref[...], preferred_element_type=F32).astype(BF16)
        w1o_ref[:, c * LANES:(c + 1) * LANES] = r[:, :LANES]
        w1o_ref[:, D_FF + c * LANES:D_FF + (c + 1) * LANES] = r[:, LANES:]
    w2o_ref[...] = w2_ref[...].astype(BF16)

    tt = u_ref.shape[0]
    rows = xre_ref.shape[0]
    bsz = rows // tt
    cw = S5_WIDTH // S5_CHUNKS
    cl = S5_LANES // S5_CHUNKS

    @pl.when(pl.program_id(0) == 0)
    def _init():
        lr = lam_ref[0:1, :]
        li = lam_ref[1:2, :]
        dt = jnp.exp(lam_ref[2:3, :])
        mag = jnp.exp(lr * dt)
        a_re = mag * jnp.cos(li * dt)
        a_im = mag * jnp.sin(li * dt)
        nr = a_re - 1.0
        den = lr * lr + li * li
        f_re = (nr * lr + a_im * li) / den
        f_im = (a_im * lr - nr * li) / den
        for c in range(S5_CHUNKS):
            fr = f_re[:, c * cl:(c + 1) * cl]
            fi = f_im[:, c * cl:(c + 1) * cl]
            b_re = bblk_ref[0, c]
            b_im = bblk_ref[1, c]
            bf_ref[0, c] = (b_re * fr - b_im * fi).astype(BF16)
            bf_ref[1, c] = (b_im * fr + b_re * fi).astype(BF16)
        a8_ref[0] = jnp.broadcast_to(a_re, (SUBLANES, S5_LANES))
        a8_ref[1] = jnp.broadcast_to(a_im, (SUBLANES, S5_LANES))
        p_ref[...] = jnp.zeros_like(p_ref)

    r_i = lax.broadcasted_iota(jnp.int32, (rows, tt), 0)
    t_i = lax.broadcasted_iota(jnp.int32, (rows, tt), 1)
    u2 = jnp.zeros((rows, S5_WIDTH), F32)
    for bb in range(bsz):
        spread = (r_i == bsz * t_i + bb).astype(BF16)
        u2 = u2 + jnp.dot(spread, u_ref[:, bb * S5_WIDTH:(bb + 1) * S5_WIDTH],
                          preferred_element_type=F32)
    ub = u2.astype(BF16)
    for c in range(S5_CHUNKS):
        uc = ub[:, c * cw:(c + 1) * cw]
        xre_ref[:, c * cl:(c + 1) * cl] = jnp.dot(uc, bf_ref[0, c], preferred_element_type=F32)
        xim_ref[:, c * cl:(c + 1) * cl] = jnp.dot(uc, bf_ref[1, c], preferred_element_type=F32)

    lower = lax.broadcasted_iota(jnp.int32, (SUBLANES, S5_LC), 0) < (SUBLANES // 2)
    for j in range(S5_LANES // S5_LC):
        sl = slice(j * S5_LC, (j + 1) * S5_LC)
        ar = a8_ref[0, :, sl]
        ai = a8_ref[1, :, sl]

        def step(k, carry, sl=sl, ar=ar, ai=ai):
            pr, pi = carry
            r0 = pl.multiple_of(k * SUBLANES, SUBLANES)
            xr = xre_ref[pl.ds(r0, SUBLANES), sl]
            xi = xim_ref[pl.ds(r0, SUBLANES), sl]
            s1r = ar * pr - ai * pi + xr
            s1i = ar * pi + ai * pr + xi
            tr = pltpu.roll(s1r, SUBLANES // 2, 0)
            ti = pltpu.roll(s1i, SUBLANES // 2, 0)
            s2r = ar * tr - ai * ti + xr
            s2i = ar * ti + ai * tr + xi
            xre_ref[pl.ds(r0, SUBLANES), sl] = jnp.where(lower, s1r, s2r)
            xim_ref[pl.ds(r0, SUBLANES), sl] = jnp.where(lower, s1i, s2i)
            return pltpu.roll(s2r, SUBLANES // 2, 0), pltpu.roll(s2i, SUBLANES // 2, 0)

        pr, pi = lax.fori_loop(0, rows // SUBLANES, step, (p_ref[0, :, sl], p_ref[1, :, sl]),
                               unroll=2)
        p_ref[0, :, sl] = pr
        p_ref[1, :, sl] = pi

    y = jnp.concatenate([
        jnp.dot(xre_ref[:, c * cl:(c + 1) * cl].astype(BF16), cblk_ref[0, c],
                preferred_element_type=F32)
        - jnp.dot(xim_ref[:, c * cl:(c + 1) * cl].astype(BF16), cblk_ref[1, c],
                  preferred_element_type=F32)
        for c in range(S5_CHUNKS)], axis=1)
    y = y + dvec_ref[0:1, :] * u2
    z = jax.nn.gelu(y)
    gate = jax.nn.sigmoid(
        jnp.dot(z.astype(BF16), wglu_ref[...], preferred_element_type=F32) + dvec_ref[1:2, :])
    out = (z * gate).astype(BF16)
    t_o = lax.broadcasted_iota(jnp.int32, (tt, rows), 0)
    r_o = lax.broadcasted_iota(jnp.int32, (tt, rows), 1)
    for bb in range(bsz):
        pick = (r_o == bsz * t_o + bb).astype(BF16)
        o_ref[:, bb * S5_WIDTH:(bb + 1) * S5_WIDTH] = jnp.dot(
            pick, out, preferred_element_type=F32).astype(BF16)


def _s5(u, lam, bblk, cblk, dvec, wglu, bsz, w1, w2):
    seqlen = u.shape[0]
    n_steps = 2 * N_EXPERTS
    tt = seqlen // n_steps
    rows = tt * bsz
    assert tt * n_steps == seqlen and rows % (2 * SUBLANES) == 0 and SUBLANES % bsz == 0
    kr = N_EXPERTS * D_MODEL // n_steps
    const2 = lambda i: (0, 0)
    const4 = lambda i: (0, 0, 0, 0)
    cw = S5_WIDTH // S5_CHUNKS
    cl = S5_LANES // S5_CHUNKS
    i_p = lax.broadcasted_iota(jnp.int32, (2 * LANES, 2 * LANES), 0)
    j_p = lax.broadcasted_iota(jnp.int32, (2 * LANES, 2 * LANES), 1)
    perm = (i_p == jnp.where(j_p < LANES, 2 * j_p, 2 * (j_p - LANES) + 1)).astype(BF16)
    y, w1p, w2b = pl.pallas_call(
        _s5_body,
        grid=(n_steps,),
        in_specs=[
            pl.BlockSpec((tt, bsz * S5_WIDTH), lambda i: (i, 0)),
            pl.BlockSpec((3, S5_LANES), const2),
            pl.BlockSpec((2, S5_CHUNKS, cw, cl), const4),
            pl.BlockSpec((2, S5_CHUNKS, cl, cw), const4),
            pl.BlockSpec((2, S5_WIDTH), const2),
            pl.BlockSpec((S5_WIDTH, S5_WIDTH), const2),
            pl.BlockSpec((kr, 2 * D_FF), lambda i: (i, 0)),
            pl.BlockSpec((2 * LANES, 2 * LANES), const2),
            pl.BlockSpec((kr, D_MODEL), lambda i: (i, 0)),
        ],
        out_specs=(
            pl.BlockSpec((tt, bsz * S5_WIDTH), lambda i: (i, 0)),
            pl.BlockSpec((kr, 2 * D_FF), lambda i: (i, 0)),
            pl.BlockSpec((kr, D_MODEL), lambda i: (i, 0)),
        ),
        out_shape=(
            jax.ShapeDtypeStruct((seqlen, bsz * S5_WIDTH), BF16),
            jax.ShapeDtypeStruct((N_EXPERTS * D_MODEL, 2 * D_FF), BF16),
            jax.ShapeDtypeStruct((N_EXPERTS * D_FF, D_MODEL), BF16),
        ),
        scratch_shapes=[
            pltpu.VMEM((2, S5_CHUNKS, cw, cl), BF16),
            pltpu.VMEM((2, SUBLANES, S5_LANES), F32),
            pltpu.VMEM((2, SUBLANES, S5_LANES), F32),
            pltpu.VMEM((rows, S5_LANES), F32),
            pltpu.VMEM((rows, S5_LANES), F32),
        ],
        compiler_params=pltpu.CompilerParams(
            dimension_semantics=("arbitrary",), vmem_limit_bytes=VMEM_LIMIT),
        name="s5_mixer",
    )(u, lam, bblk, cblk, dvec, wglu, w1.reshape(N_EXPERTS * D_MODEL, 2 * D_FF), perm,
      w2.reshape(N_EXPERTS * D_FF, D_MODEL))
    return (y, w1p.reshape(N_EXPERTS, D_MODEL, 2 * D_FF), w2b.reshape(N_EXPERTS, D_FF, D_MODEL))


def _ret_body(q_ref, k_ref, v_ref, g_ref, intra_ref, qd_ref, kd_ref, cd_ref, gn_ref, o_ref,
              st_ref):
    tr = q_ref.shape[1]

    @pl.when(pl.program_id(1) == 0)
    def _init():
        st_ref[...] = jnp.zeros_like(st_ref)

    for h in range(RET_HEADS):
        qs = slice(h * RET_DK, (h + 1) * RET_DK)
        vs = slice(h * RET_DV, (h + 1) * RET_DV)
        for c in range(tr // RET_CHUNK):
            rs = slice(c * RET_CHUNK, (c + 1) * RET_CHUNK)
            qc = q_ref[0, rs, qs]
            kc = k_ref[0, rs, qs]
            vc = v_ref[0, rs, vs]
            scores = lax.dot_general(qc, kc, (((1,), (1,)), ((), ())),
                                     preferred_element_type=F32) * intra_ref[h]
            inner = jnp.dot(scores.astype(BF16), vc, preferred_element_type=F32)
            st = st_ref[h]
            qdec = (qc.astype(F32) * qd_ref[h]).astype(BF16)
            cross = jnp.dot(qdec, st.astype(BF16), preferred_element_type=F32)
            kdec_t = (kc.astype(F32) * kd_ref[h]).T.astype(BF16)
            st_ref[h] = cd_ref[h] * st + jnp.dot(kdec_t, vc, preferred_element_type=F32)
            ret = inner + cross
            mu = jnp.mean(ret, axis=-1, keepdims=True)
            dev = ret - mu
            var = jnp.mean(dev * dev, axis=-1, keepdims=True)
            yn = dev * lax.rsqrt(var + NORM_EPS) * gn_ref[0:1, vs]
            o_ref[0, rs, vs] = (jax.nn.silu(g_ref[0, rs, vs].astype(F32)) * yn).astype(BF16)


def _retention(q, k, v, g, intra, qd, kd, cd, gn):
    bsz, seqlen, _ = q.shape
    tr = min(TR_RET, seqlen)
    row = lambda b, t: (b, t, 0)
    const3 = lambda b, t: (0, 0, 0)
    return pl.pallas_call(
        _ret_body,
        grid=(bsz, seqlen // tr),
        in_specs=[
            pl.BlockSpec((1, tr, RET_QK_WIDTH), row),
            pl.BlockSpec((1, tr, RET_QK_WIDTH), row),
            pl.BlockSpec((1, tr, RET_V_WIDTH), row),
            pl.BlockSpec((1, tr, RET_V_WIDTH), row),
            pl.BlockSpec((RET_HEADS, RET_CHUNK, RET_CHUNK), const3),
            pl.BlockSpec((RET_HEADS, RET_CHUNK, RET_DK), const3),
            pl.BlockSpec((RET_HEADS, RET_CHUNK, RET_DK), const3),
            pl.BlockSpec((RET_HEADS, 1, RET_DV), const3),
            pl.BlockSpec((1, RET_V_WIDTH), lambda b, t: (0, 0)),
        ],
        out_specs=pl.BlockSpec((1, tr, RET_V_WIDTH), row),
        out_shape=jax.ShapeDtypeStruct((bsz, seqlen, RET_V_WIDTH), BF16),
        scratch_shapes=[pltpu.VMEM((RET_HEADS, RET_DK, RET_DV), F32)],
        compiler_params=pltpu.CompilerParams(
            dimension_semantics=("arbitrary", "arbitrary"), vmem_limit_bytes=VMEM_LIMIT),
        name="retention",
    )(q, k, v, g, intra, qd, kd, cd, gn)


def _merge_body(x_ref, ys_ref, rg_ref, ga_ref, gb_ref, wbs_ref, wbr_ref, wo_ref, gf_ref,
                rw_ref, rb_ref,
                x1_ref, ri_ref, gt_ref, cnt_ref, xrows_ref,
                run_ref, hrows_ref, dvm_ref, dsm_ref, sem, isem):
    tm = x_ref.shape[1]
    cap = (xrows_ref.shape[0] - TOP_K * tm) // N_EXPERTS
    step = pl.program_id(0) * pl.num_programs(1) + pl.program_id(1)
    last = pl.num_programs(0) * pl.num_programs(1) - 1
    slot = step % 2

    def dests_to_smem():
        return pltpu.make_async_copy(dvm_ref, dsm_ref, isem)

    @pl.when(step == 0)
    def _init():
        run_ref[...] = jnp.zeros_like(run_ref)
        hrows_ref[1] = jnp.zeros((tm, SUBLANES, LANES), F32)
        k_i = lax.broadcasted_iota(jnp.int32, (SUBLANES, tm), 0)
        r_j = lax.broadcasted_iota(jnp.int32, (SUBLANES, tm), 1)
        dvm_ref[...] = N_EXPERTS * cap + TOP_K * r_j + jnp.minimum(k_i, TOP_K - 1)
        dests_to_smem().start()
        dests_to_smem().wait()

    pending = hrows_ref.at[1 - slot]
    for r in range(tm):
        for kk in range(TOP_K):
            pltpu.make_async_copy(pending.at[r], xrows_ref.at[dsm_ref[kk, r]],
                                  sem).start(priority=kk % 2)

    y_a = jnp.dot(ys_ref[...], wbs_ref[...], preferred_element_type=F32)
    y_b = jnp.dot(rg_ref[0], wbr_ref[...], preferred_element_type=F32)
    merged = _sigmoid(ga_ref[0].astype(F32)) * y_a + _sigmoid(gb_ref[0].astype(F32)) * y_b
    x1 = x_ref[0] + jnp.dot(merged.astype(BF16), wo_ref[...], preferred_element_type=F32)
    x1_ref[0] = x1
    h2 = _rms(x1, gf_ref[...])

    lane = lax.broadcasted_iota(jnp.int32, (tm, LANES), 1)
    h_hi = h2.astype(BF16)
    h_lo = (h2 - h_hi.astype(F32)).astype(BF16)
    hi_terms = jnp.dot(h_hi, rw_ref[...], preferred_element_type=F32)
    logits = (hi_terms[:, :LANES] + hi_terms[:, LANES:]
              + jnp.dot(h_lo, rw_ref[:, :LANES], preferred_element_type=F32)) + rb_ref[...]
    work = jnp.where(lane < N_EXPERTS, logits, -jnp.inf)
    lane_f = lane.astype(F32)
    vals, hots, idxs = [], [], []
    for kk in range(TOP_K):
        m = jnp.max(work, axis=-1, keepdims=True)
        idx = jnp.min(jnp.where(work == m, lane_f, float(LANES)), axis=-1, keepdims=True)
        hot = lane_f == idx
        vals.append(m)
        hots.append(hot)
        idxs.append(idx)
        work = jnp.where(hot, -jnp.inf, work)
    exps = [jnp.exp(v - vals[0]) for v in vals]
    denom = exps[0] + exps[1] + exps[2] + exps[3]
    gt = jnp.zeros((tm, LANES), F32)
    for kk in range(TOP_K):
        gt = jnp.where(lane == kk, exps[kk] / denom, gt)
    gt_ref[0] = gt[:, :2 * TOP_K]

    onehot = [h.astype(F32) for h in hots]
    tot = onehot[0] + onehot[1] + onehot[2] + onehot[3]
    r_i = lax.broadcasted_iota(jnp.int32, (tm, tm), 0)
    c_i = lax.broadcasted_iota(jnp.int32, (tm, tm), 1)
    tril = (c_i < r_i).astype(BF16)
    base = jnp.dot(tril, tot.astype(BF16), preferred_element_type=F32) + run_ref[...]
    dmat = jnp.zeros((tm, LANES), F32)
    for kk in range(TOP_K):
        rank = jnp.sum(onehot[kk] * base, axis=-1, keepdims=True)
        dmat = jnp.where(lane == kk, idxs[kk] * float(cap) + rank, dmat)
    ri_ref[0] = dmat.astype(jnp.int32)[:, :2 * TOP_K]
    run_ref[...] = run_ref[...] + jnp.sum(tot, axis=0, keepdims=True)
    cnt_ref[...] = run_ref[...]

    hrows_ref[slot] = h2.reshape(tm, SUBLANES, LANES)

    def drain():
        def wait(r, c):
            for kk in range(TOP_K):
                pltpu.make_async_copy(hrows_ref.at[0, 0], xrows_ref.at[0], sem).wait()
            return c
        lax.fori_loop(0, tm, wait, 0, unroll=8)

    drain()
    dvm_ref[...] = dmat.T[:SUBLANES, :].astype(jnp.int32)
    dests_to_smem().start()
    dests_to_smem().wait()

    @pl.when(step == last)
    def _flush():
        mine = hrows_ref.at[slot]

        def start(r, c):
            for kk in range(TOP_K):
                pltpu.make_async_copy(mine.at[r], xrows_ref.at[dsm_ref[kk, r]],
                                      sem).start(priority=kk % 2)
            return c

        lax.fori_loop(0, tm, start, 0, unroll=8)
        drain()


def _merge(x, ys5, retg, ga, gb, wbs, wbr, wo, gf, rw, rb):
    bsz, seqlen, _ = x.shape
    n_tok = bsz * seqlen
    tm = min(TM_PROJ, seqlen)
    cap = n_tok
    row = lambda b, t: (b, t, 0)
    const = lambda b, t: (0, 0)
    out_shape = (
        jax.ShapeDtypeStruct((bsz, seqlen, D_MODEL), F32),
        jax.ShapeDtypeStruct((bsz, seqlen, 2 * TOP_K), jnp.int32),
        jax.ShapeDtypeStruct((bsz, seqlen, 2 * TOP_K), F32),
        jax.ShapeDtypeStruct((1, LANES), F32),
        jax.ShapeDtypeStruct((N_EXPERTS * cap + TOP_K * tm, SUBLANES, LANES), F32),
    )
    return pl.pallas_call(
        _merge_body,
        grid=(bsz, seqlen // tm),
        in_specs=[
            pl.BlockSpec((1, tm, D_MODEL), row),
            pl.BlockSpec((tm, S5_WIDTH), lambda b, t: (t, b)),
            pl.BlockSpec((1, tm, RET_V_WIDTH), row),
            pl.BlockSpec((1, tm, D_MODEL), row),
            pl.BlockSpec((1, tm, D_MODEL), row),
            pl.BlockSpec((S5_WIDTH, D_MODEL), const),
            pl.BlockSpec((RET_V_WIDTH, D_MODEL), const),
            pl.BlockSpec((D_MODEL, D_MODEL), const),
            pl.BlockSpec((1, D_MODEL), const),
            pl.BlockSpec((D_MODEL, 2 * LANES), const),
            pl.BlockSpec((1, LANES), const),
        ],
        out_specs=(
            pl.BlockSpec((1, tm, D_MODEL), row),
            pl.BlockSpec((1, tm, 2 * TOP_K), row),
            pl.BlockSpec((1, tm, 2 * TOP_K), row),
            pl.BlockSpec((1, LANES), const),
            pl.BlockSpec(memory_space=pl.ANY),
        ),
        out_shape=out_shape,
        scratch_shapes=[
            pltpu.VMEM((1, LANES), F32),
            pltpu.VMEM((2, tm, SUBLANES, LANES), F32),
            pltpu.VMEM((SUBLANES, tm), jnp.int32),
            pltpu.SMEM((SUBLANES, tm), jnp.int32),
            pltpu.SemaphoreType.DMA(()),
            pltpu.SemaphoreType.DMA(()),
        ],
        compiler_params=pltpu.CompilerParams(
            dimension_semantics=("arbitrary", "arbitrary"), has_side_effects=True,
            vmem_limit_bytes=VMEM_LIMIT),
        name="merge_router_dispatch",
    )(x, ys5, retg, ga, gb, wbs, wbr, wo, gf, rw, rb)


def _expert_body(be_ref, brow_ref, bvalid_ref, na_ref, x_ref, w1_ref, b1_ref, w2_ref, b2_ref,
                 y_ref):
    del be_ref, brow_ref
    i = pl.program_id(0)

    @pl.when(i < na_ref[0])
    def _compute():
        bm = x_ref.shape[0]
        row = lax.broadcasted_iota(jnp.int32, (bm, D_MODEL), 0)
        xb = jnp.where(row < bvalid_ref[i], x_ref[...].reshape(bm, D_MODEL), 0.0).astype(BF16)
        gu = jnp.dot(xb, w1_ref[0], preferred_element_type=F32) + b1_ref[0]
        x_glu = jnp.minimum(gu[:, :D_FF], SWIGLU_LIMIT)
        x_lin = jnp.clip(gu[:, D_FF:], -SWIGLU_LIMIT, SWIGLU_LIMIT)
        act = x_glu * jax.nn.sigmoid(SWIGLU_ALPHA * x_glu) * (x_lin + 1.0)
        y = jnp.dot(act.astype(BF16), w2_ref[0], preferred_element_type=F32) + b2_ref[0]
        y_ref[...] = y.reshape(bm, SUBLANES, LANES)


def _experts(block_expert, block_row, block_valid, n_active, x_rows, w1, b1, w2, b2, bm):
    n_blocks = block_expert.shape[0]
    xmap = lambda i, be, br, bv, na: (br[i], 0, 0)
    wmap = lambda i, be, br, bv, na: (be[i], 0, 0)
    grid_spec = pltpu.PrefetchScalarGridSpec(
        num_scalar_prefetch=4,
        grid=(n_blocks,),
        in_specs=[
            pl.BlockSpec((bm, SUBLANES, LANES), xmap),
            pl.BlockSpec((1, D_MODEL, 2 * D_FF), wmap),
            pl.BlockSpec((1, 1, 2 * D_FF), wmap),
            pl.BlockSpec((1, D_FF, D_MODEL), wmap),
            pl.BlockSpec((1, 1, D_MODEL), wmap),
        ],
        out_specs=pl.BlockSpec((bm, SUBLANES, LANES), xmap),
    )
    return pl.pallas_call(
        _expert_body,
        grid_spec=grid_spec,
        out_shape=jax.ShapeDtypeStruct(x_rows.shape, F32),
        compiler_params=pltpu.CompilerParams(
            dimension_semantics=("arbitrary",), vmem_limit_bytes=VMEM_LIMIT),
        name="expert_ffn",
    )(block_expert, block_row, block_valid, n_active, x_rows, w1, b1, w2, b2)


def _combine_body(dest_ref, nxt_ref, gt_ref, x1_ref, gfin_ref, yrows_ref, o_ref, ybuf_ref, sems):
    td = x1_ref.shape[0]
    step = pl.program_id(0)
    slot = step % 2

    def issue(idx_ref, which):
        def start(r, c):
            for kk in range(TOP_K):
                d = idx_ref[0, 0, r * TOP_K + kk]
                pltpu.make_async_copy(yrows_ref.at[d], ybuf_ref.at[which, kk, r],
                                      sems.at[which]).start(priority=kk % 2)
            return c
        lax.fori_loop(0, td, start, 0, unroll=8)

    @pl.when(step == 0)
    def _first():
        issue(dest_ref, 0)

    for par in range(2):
        @pl.when((step < pl.num_programs(0) - 1) & (slot == par))
        def _prefetch(par=par):
            issue(nxt_ref, 1 - par)

    def wait(r, c):
        for kk in range(TOP_K):
            pltpu.make_async_copy(yrows_ref.at[0], ybuf_ref.at[0, kk, 0], sems.at[slot]).wait()
        return c

    lax.fori_loop(0, td, wait, 0, unroll=8)
    gt = gt_ref[...]
    acc = x1_ref[...]
    for kk in range(TOP_K):
        acc = acc + gt[:, kk:kk + 1] * ybuf_ref[slot, kk].reshape(td, D_MODEL)
    o_ref[...] = _rms(acc, gfin_ref[...])


def _combine(dest, gates, x1, gfin, y_rows):
    n_tok = x1.shape[0]
    td = min(TD_ROWS, n_tok)
    n_steps = n_tok // td
    dest3 = dest.reshape(n_steps, 1, td * TOP_K)
    return pl.pallas_call(
        _combine_body,
        grid=(n_steps,),
        in_specs=[
            pl.BlockSpec((1, 1, td * TOP_K), lambda i: (i, 0, 0), memory_space=pltpu.SMEM),
            pl.BlockSpec((1, 1, td * TOP_K), lambda i: (jnp.minimum(i + 1, n_steps - 1), 0, 0),
                         memory_space=pltpu.SMEM),
            pl.BlockSpec((td, 2 * TOP_K), lambda i: (i, 0)),
            pl.BlockSpec((td, D_MODEL), lambda i: (i, 0)),
            pl.BlockSpec((1, D_MODEL), lambda i: (0, 0)),
            pl.BlockSpec(memory_space=pl.ANY),
        ],
        out_specs=pl.BlockSpec((td, D_MODEL), lambda i: (i, 0)),
        out_shape=jax.ShapeDtypeStruct((n_tok, D_MODEL), F32),
        scratch_shapes=[pltpu.VMEM((2, TOP_K, td, SUBLANES, LANES), F32),
                        pltpu.SemaphoreType.DMA((2,))],
        compiler_params=pltpu.CompilerParams(
            dimension_semantics=("arbitrary",), vmem_limit_bytes=VMEM_LIMIT),
        name="combine",
    )(dest3, dest3, gates, x1, gfin, y_rows)


def _block_diag(p):
    g, a, b = p.shape
    eye = jnp.eye(g, dtype=p.dtype)
    return (p[:, :, None, :] * eye[:, None, :, None]).reshape(g * a, g * b)


def _retention_tables():
    log_g = jnp.log1p(-(2.0 ** (-5.0 - jnp.arange(RET_HEADS, dtype=F32))))
    pos = jnp.arange(RET_CHUNK, dtype=F32)
    rel = pos[:, None] - pos[None, :]
    intra = jnp.where(rel >= 0, jnp.exp(log_g[:, None, None] * jnp.maximum(rel, 0.0)), 0.0)
    q_decay = jnp.exp(log_g[:, None] * (pos + 1.0))
    k_decay = jnp.exp(log_g[:, None] * (RET_CHUNK - 1.0 - pos))
    chunk_decay = jnp.exp(log_g * RET_CHUNK)
    qd = jnp.broadcast_to(q_decay[:, :, None], (RET_HEADS, RET_CHUNK, RET_DK))
    kd = jnp.broadcast_to(k_decay[:, :, None], (RET_HEADS, RET_CHUNK, RET_DK))
    cd = jnp.broadcast_to(chunk_decay[:, None, None], (RET_HEADS, 1, RET_DV))
    return intra, qd, kd, cd


def _layer(x, positions, norm_mix_g, w_in, lam_re, lam_im, log_dt, b_re, b_im, c_re, c_im,
           s5_d, w_glu, b_glu, ret_gn_g, w_branch_s5, w_branch_ret, w_out, norm_ffn_g,
           router_w, router_b, w1, b1, w2, b2, g_out):
    bsz, seqlen, _ = x.shape
    n_tok = bsz * seqlen

    half = RET_DK // 2
    inv_freq = ROPE_BASE ** (-jnp.arange(half, dtype=F32) / half)
    rope = jnp.stack([jnp.concatenate([inv_freq, inv_freq]),
                      jnp.concatenate([-jnp.ones((half,), F32), jnp.ones((half,), F32)])])
    pos128 = jnp.broadcast_to(positions.astype(F32)[..., None], (bsz, seqlen, LANES))
    u, q, k, v, g_ret, gate_a, gate_b = _inproj(
        x, pos128, rope, norm_mix_g.reshape(1, D_MODEL), w_in.astype(BF16))

    lam = jnp.stack([lam_re.reshape(-1), lam_im.reshape(-1),
                     jnp.repeat(log_dt, S5_STATE)]).astype(F32)
    gpc = S5_GROUPS // S5_CHUNKS

    def chunked_block_diag(p):
        return jax.vmap(_block_diag)(p.reshape(S5_CHUNKS, gpc, p.shape[1], p.shape[2]))

    bblk = jnp.stack([chunked_block_diag(jnp.swapaxes(b_re, 1, 2)),
                      chunked_block_diag(jnp.swapaxes(b_im, 1, 2))])
    cblk = jnp.stack([chunked_block_diag(jnp.swapaxes(c_re, 1, 2)),
                      chunked_block_diag(jnp.swapaxes(c_im, 1, 2))]).astype(BF16)
    dvec = jnp.stack([s5_d, b_glu])
    ys5, w1p, w2b = _s5(u, lam, bblk, cblk, dvec, w_glu.astype(BF16), bsz, w1, w2)

    intra, qd, kd, cd = _retention_tables()
    retg = _retention(q, k, v, g_ret, intra, qd, kd, cd, ret_gn_g.reshape(1, RET_V_WIDTH))

    rw_hi = router_w.astype(BF16)
    rw_lo = (router_w - rw_hi.astype(F32)).astype(BF16)
    rw = (jnp.zeros((D_MODEL, 2 * LANES), BF16).at[:, :N_EXPERTS].set(rw_hi)
          .at[:, LANES:LANES + N_EXPERTS].set(rw_lo))
    rb = jnp.zeros((1, LANES), F32).at[0, :N_EXPERTS].set(router_b)
    x1, route, gates, counts, x_rows = _merge(
        x, ys5, retg, gate_a, gate_b, w_branch_s5.astype(BF16), w_branch_ret.astype(BF16),
        w_out.astype(BF16), norm_ffn_g.reshape(1, D_MODEL), rw, rb)

    bm = BM_EXPERT
    cap = n_tok
    n_blocks = n_tok * TOP_K // bm + N_EXPERTS
    dest = route.reshape(n_tok, 2 * TOP_K)[:, :TOP_K]
    cnt = counts[0, :N_EXPERTS].astype(jnp.int32)
    nblk = (cnt + bm - 1) // bm
    bend = jnp.cumsum(nblk)
    n_active = bend[N_EXPERTS - 1:]
    blk = jnp.minimum(jnp.arange(n_blocks, dtype=jnp.int32), n_active[0] - 1)
    block_expert = jnp.sum((bend[None, :] <= blk[:, None]).astype(jnp.int32), axis=1)
    sel = (jnp.arange(N_EXPERTS, dtype=jnp.int32)[None, :] == block_expert[:, None]).astype(jnp.int32)
    j_in = blk - jnp.sum(sel * (bend - nblk)[None, :], axis=1)
    block_row = block_expert * (cap // bm) + j_in
    block_valid = jnp.clip(jnp.sum(sel * cnt[None, :], axis=1) - j_in * bm, 0, bm)

    b1p = jnp.concatenate([b1[:, 0::2], b1[:, 1::2]], axis=-1).reshape(N_EXPERTS, 1, 2 * D_FF)
    y_rows = _experts(block_expert.astype(jnp.int32), block_row.astype(jnp.int32),
                      block_valid.astype(jnp.int32), n_active.astype(jnp.int32), x_rows, w1p, b1p,
                      w2b, b2.reshape(N_EXPERTS, 1, D_MODEL), bm)
    out = _combine(dest, gates.reshape(n_tok, 2 * TOP_K), x1.reshape(n_tok, D_MODEL),
                   g_out.reshape(1, D_MODEL), y_rows)
    return out.reshape(bsz, seqlen, D_MODEL)


def kernel(x, positions, norm_mix_g, w_in, s5_lambda_re, s5_lambda_im, s5_log_dt, s5_b_re, s5_b_im, s5_c_re, s5_c_im, s5_d, s5_w_glu, s5_b_glu, ret_gn_g, w_branch_s5, w_branch_ret, w_out, norm_ffn_g, router_w, router_b, expert_w1, expert_b1, expert_w2, expert_b2, norm_final_g):
    assert norm_mix_g.shape[0] == 1, "single-layer trunk"
    return _layer(x, positions, norm_mix_g[0], w_in[0], s5_lambda_re[0], s5_lambda_im[0],
                  s5_log_dt[0], s5_b_re[0], s5_b_im[0], s5_c_re[0], s5_c_im[0], s5_d[0],
                  s5_w_glu[0], s5_b_glu[0], ret_gn_g[0], w_branch_s5[0], w_branch_ret[0],
                  w_out[0], norm_ffn_g[0], router_w[0], router_b[0], expert_w1[0], expert_b1[0],
                  expert_w2[0], expert_b2[0], norm_final_g)
```

```python
import jax
import jax.numpy as jnp
from jax import lax
from jax.experimental import pallas as pl
from jax.experimental.pallas import tpu as pltpu

F32 = jnp.float32
BF16 = jnp.bfloat16

D_MODEL = 1024
S5_WIDTH = 512
S5_GROUP = 16
S5_GROUPS = 32
S5_STATE = 64
S5_LANES = S5_GROUPS * S5_STATE
RET_HEADS = 4
RET_DK = 128
RET_DV = 256
RET_QK_WIDTH = RET_HEADS * RET_DK
RET_V_WIDTH = RET_HEADS * RET_DV
RET_CHUNK = 128
ROPE_BASE = 10000.0
N_EXPERTS = 32
TOP_K = 4
D_FF = 1024
SWIGLU_ALPHA = 1.702
SWIGLU_LIMIT = 7.0
NORM_EPS = 1e-5
IN_WIDTH = S5_WIDTH + 2 * RET_QK_WIDTH + 2 * RET_V_WIDTH + 2 * D_MODEL
OFF_Q = S5_WIDTH
OFF_V = OFF_Q + 2 * RET_QK_WIDTH
OFF_G = OFF_V + RET_V_WIDTH
OFF_GA = OFF_G + RET_V_WIDTH
OFF_GB = OFF_GA + D_MODEL

LANES = 128
SUBLANES = 8
VMEM_LIMIT = 56 * 1024 * 1024

TM_PROJ = 512
S5_LC = 512
S5_CHUNKS = 4
TR_RET = 512
BM_EXPERT = 512
TD_ROWS = 512


def _rms(x, g):
    return x * lax.rsqrt(jnp.mean(x * x, axis=-1, keepdims=True) + NORM_EPS) * g


def _sigmoid(x):
    return 0.5 * jnp.tanh(0.5 * x) + 0.5


def _inproj_body(x_ref, pos_ref, rope_ref, g_ref, w_ref,
                 u_ref, q_ref, k_ref, v_ref, gr_ref, ga_ref, gb_ref):
    hb = _rms(x_ref[0], g_ref[...]).astype(BF16)

    def proj(lo, width):
        return jnp.dot(hb, w_ref[:, lo:lo + width], preferred_element_type=F32)

    u_ref[...] = proj(0, S5_WIDTH).astype(BF16)
    ang = pos_ref[0] * rope_ref[0:1, :]
    cos = jnp.cos(ang)
    sin = jnp.sin(ang) * rope_ref[1:2, :]
    qk = proj(OFF_Q, 2 * RET_QK_WIDTH)
    for h in range(RET_HEADS):
        qh = qk[:, h * RET_DK:(h + 1) * RET_DK]
        q_ref[0, :, h * RET_DK:(h + 1) * RET_DK] = (
            qh * cos + pltpu.roll(qh, RET_DK // 2, 1) * sin).astype(BF16)
        kh = qk[:, RET_QK_WIDTH + h * RET_DK:RET_QK_WIDTH + (h + 1) * RET_DK]
        k_ref[0, :, h * RET_DK:(h + 1) * RET_DK] = (
            (kh * cos + pltpu.roll(kh, RET_DK // 2, 1) * sin) * (RET_DK ** -0.5)).astype(BF16)
    v_ref[0] = proj(OFF_V, RET_V_WIDTH).astype(BF16)
    gr_ref[0] = proj(OFF_G, RET_V_WIDTH).astype(BF16)
    ga_ref[0] = proj(OFF_GA, D_MODEL).astype(BF16)
    gb_ref[0] = proj(OFF_GB, D_MODEL).astype(BF16)


def _inproj(x, pos128, rope, g, w_bf):
    bsz, seqlen, _ = x.shape
    tm = min(TM_PROJ, seqlen)
    grid = (bsz, seqlen // tm)
    row = lambda b, t: (b, t, 0)
    const = lambda b, t: (0, 0)
    out_shape = (
        jax.ShapeDtypeStruct((seqlen, bsz * S5_WIDTH), BF16),
        jax.ShapeDtypeStruct((bsz, seqlen, RET_QK_WIDTH), BF16),
        jax.ShapeDtypeStruct((bsz, seqlen, RET_QK_WIDTH), BF16),
        jax.ShapeDtypeStruct((bsz, seqlen, RET_V_WIDTH), BF16),
        jax.ShapeDtypeStruct((bsz, seqlen, RET_V_WIDTH), BF16),
        jax.ShapeDtypeStruct((bsz, seqlen, D_MODEL), BF16),
        jax.ShapeDtypeStruct((bsz, seqlen, D_MODEL), BF16),
    )
    return pl.pallas_call(
        _inproj_body,
        grid=grid,
        in_specs=[
            pl.BlockSpec((1, tm, D_MODEL), row),
            pl.BlockSpec((1, tm, LANES), row),
            pl.BlockSpec((2, LANES), const),
            pl.BlockSpec((1, D_MODEL), const),
            pl.BlockSpec((D_MODEL, IN_WIDTH), const, pipeline_mode=pl.Buffered(1)),
        ],
        out_specs=(
            pl.BlockSpec((tm, S5_WIDTH), lambda b, t: (t, b)),
            pl.BlockSpec((1, tm, RET_QK_WIDTH), row),
            pl.BlockSpec((1, tm, RET_QK_WIDTH), row),
            pl.BlockSpec((1, tm, RET_V_WIDTH), row),
            pl.BlockSpec((1, tm, RET_V_WIDTH), row),
            pl.BlockSpec((1, tm, D_MODEL), row),
            pl.BlockSpec((1, tm, D_MODEL), row),
        ),
        out_shape=out_shape,
        compiler_params=pltpu.CompilerParams(
            dimension_semantics=("arbitrary", "arbitrary"), vmem_limit_bytes=VMEM_LIMIT),
        name="inproj",
    )(x, pos128, rope, g, w_bf)


def _s5_body(u_ref, lam_ref, bblk_ref, cblk_ref, dvec_ref, wglu_ref, w1_ref, perm_ref, w2_ref,
             o_ref, w1o_ref, w2o_ref, bf_ref, a8_ref, p_ref, xre_ref, xim_ref):
    for c in range(D_FF // LANES):
        wc = w1_ref[:, c * 2 * LANES:(c + 1) * 2 * LANES].astype(BF16)
        r = jnp.dot(wc, perm_ref[...], preferred_element_type=F32).astype(BF16)
        w1o_ref[:, c * LANES:(c + 1) * LANES] = r[:, :LANES]
        w1o_ref[:, D_FF + c * LANES:D_FF + (c + 1) * LANES] = r[:, LANES:]
    w2o_ref[...] = w2_ref[...].astype(BF16)

    tt = u_ref.shape[0]
    rows = xre_ref.shape[0]
    bsz = rows // tt
    cw = S5_WIDTH // S5_CHUNKS
    cl = S5_LANES // S5_CHUNKS

    @pl.when(pl.program_id(0) == 0)
    def _init():
        lr = lam_ref[0:1, :]
        li = lam_ref[1:2, :]
        dt = jnp.exp(lam_ref[2:3, :])
        mag = jnp.exp(lr * dt)
        a_re = mag * jnp.cos(li * dt)
        a_im = mag * jnp.sin(li * dt)
        nr = a_re - 1.0
        den = lr * lr + li * li
        f_re = (nr * lr + a_im * li) / den
        f_im = (a_im * lr - nr * li) / den
        for c in range(S5_CHUNKS):
            fr = f_re[:, c * cl:(c + 1) * cl]
            fi = f_im[:, c * cl:(c + 1) * cl]
            b_re = bblk_ref[0, c]
            b_im = bblk_ref[1, c]
            bf_ref[0, c] = (b_re * fr - b_im * fi).astype(BF16)
            bf_ref[1, c] = (b_im * fr + b_re * fi).astype(BF16)
        a8_ref[0] = jnp.broadcast_to(a_re, (SUBLANES, S5_LANES))
        a8_ref[1] = jnp.broadcast_to(a_im, (SUBLANES, S5_LANES))
        p_ref[...] = jnp.zeros_like(p_ref)

    r_i = lax.broadcasted_iota(jnp.int32, (rows, tt), 0)
    t_i = lax.broadcasted_iota(jnp.int32, (rows, tt), 1)
    u2 = jnp.zeros((rows, S5_WIDTH), F32)
    for bb in range(bsz):
        spread = (r_i == bsz * t_i + bb).astype(BF16)
        u2 = u2 + jnp.dot(spread, u_ref[:, bb * S5_WIDTH:(bb + 1) * S5_WIDTH],
                          preferred_element_type=F32)
    ub = u2.astype(BF16)
    for c in range(S5_CHUNKS):
        uc = ub[:, c * cw:(c + 1) * cw]
        xre_ref[:, c * cl:(c + 1) * cl] = jnp.dot(uc, bf_ref[0, c], preferred_element_type=F32)
        xim_ref[:, c * cl:(c + 1) * cl] = jnp.dot(uc, bf_ref[1, c], preferred_element_type=F32)

    lower = lax.broadcasted_iota(jnp.int32, (SUBLANES, S5_LC), 0) < (SUBLANES // 2)
    for j in range(S5_LANES // S5_LC):
        sl = slice(j * S5_LC, (j + 1) * S5_LC)
        ar = a8_ref[0, :, sl]
        ai = a8_ref[1, :, sl]

        def step(k, carry, sl=sl, ar=ar, ai=ai):
            pr, pi = carry
            r0 = pl.multiple_of(k * SUBLANES, SUBLANES)
            xr = xre_ref[pl.ds(r0, SUBLANES), sl]
            xi = xim_ref[pl.ds(r0, SUBLANES), sl]
            s1r = ar * pr - ai * pi + xr
            s1i = ar * pi + ai * pr + xi
            tr = pltpu.roll(s1r, SUBLANES // 2, 0)
            ti = pltpu.roll(s1i, SUBLANES // 2, 0)
            s2r = ar * tr - ai * ti + xr
            s2i = ar * ti + ai * tr + xi
            xre_ref[pl.ds(r0, SUBLANES), sl] = jnp.where(lower, s1r, s2r)
            xim_ref[pl.ds(r0, SUBLANES), sl] = jnp.where(lower, s1i, s2i)
            return pltpu.roll(s2r, SUBLANES // 2, 0), pltpu.roll(s2i, SUBLANES // 2, 0)

        pr, pi = lax.fori_loop(0, rows // SUBLANES, step, (p_ref[0, :, sl], p_ref[1, :, sl]),
                               unroll=2)
        p_ref[0, :, sl] = pr
        p_ref[1, :, sl] = pi

    y = jnp.concatenate([
        jnp.dot(xre_ref[:, c * cl:(c + 1) * cl].astype(BF16), cblk_ref[0, c],
                preferred_element_type=F32)
        - jnp.dot(xim_ref[:, c * cl:(c + 1) * cl].astype(BF16), cblk_ref[1, c],
                  preferred_element_type=F32)
        for c in range(S5_CHUNKS)], axis=1)
    y = y + dvec_ref[0:1, :] * u2
    z = jax.nn.gelu(y)
    gate = jax.nn.sigmoid(
        jnp.dot(z.astype(BF16), wglu_ref[...], preferred_element_type=F32) + dvec_ref[1:2, :])
    out = (z * gate).astype(BF16)
    t_o = lax.broadcasted_iota(jnp.int32, (tt, rows), 0)
    r_o = lax.broadcasted_iota(jnp.int32, (tt, rows), 1)
    for bb in range(bsz):
        pick = (r_o == bsz * t_o + bb).astype(BF16)
        o_ref[:, bb * S5_WIDTH:(bb + 1) * S5_WIDTH] = jnp.dot(
            pick, out, preferred_element_type=F32).astype(BF16)


def _s5(u, lam, bblk, cblk, dvec, wglu, bsz, w1, w2):
    seqlen = u.shape[0]
    n_steps = 2 * N_EXPERTS
    tt = seqlen // n_steps
    rows = tt * bsz
    assert tt * n_steps == seqlen and rows % (2 * SUBLANES) == 0 and SUBLANES % bsz == 0
    kr = N_EXPERTS * D_MODEL // n_steps
    const2 = lambda i: (0, 0)
    const4 = lambda i: (0, 0, 0, 0)
    cw = S5_WIDTH // S5_CHUNKS
    cl = S5_LANES // S5_CHUNKS
    i_p = lax.broadcasted_iota(jnp.int32, (2 * LANES, 2 * LANES), 0)
    j_p = lax.broadcasted_iota(jnp.int32, (2 * LANES, 2 * LANES), 1)
    perm = (i_p == jnp.where(j_p < LANES, 2 * j_p, 2 * (j_p - LANES) + 1)).astype(BF16)
    y, w1p, w2b = pl.pallas_call(
        _s5_body,
        grid=(n_steps,),
        in_specs=[
            pl.BlockSpec((tt, bsz * S5_WIDTH), lambda i: (i, 0)),
            pl.BlockSpec((3, S5_LANES), const2),
            pl.BlockSpec((2, S5_CHUNKS, cw, cl), const4),
            pl.BlockSpec((2, S5_CHUNKS, cl, cw), const4),
            pl.BlockSpec((2, S5_WIDTH), const2),
            pl.BlockSpec((S5_WIDTH, S5_WIDTH), const2),
            pl.BlockSpec((kr, 2 * D_FF), lambda i: (i, 0)),
            pl.BlockSpec((2 * LANES, 2 * LANES), const2),
            pl.BlockSpec((kr, D_MODEL), lambda i: (i, 0)),
        ],
        out_specs=(
            pl.BlockSpec((tt, bsz * S5_WIDTH), lambda i: (i, 0)),
            pl.BlockSpec((kr, 2 * D_FF), lambda i: (i, 0)),
            pl.BlockSpec((kr, D_MODEL), lambda i: (i, 0)),
        ),
        out_shape=(
            jax.ShapeDtypeStruct((seqlen, bsz * S5_WIDTH), BF16),
            jax.ShapeDtypeStruct((N_EXPERTS * D_MODEL, 2 * D_FF), BF16),
            jax.ShapeDtypeStruct((N_EXPERTS * D_FF, D_MODEL), BF16),
        ),
        scratch_shapes=[
            pltpu.VMEM((2, S5_CHUNKS, cw, cl), BF16),
            pltpu.VMEM((2, SUBLANES, S5_LANES), F32),
            pltpu.VMEM((2, SUBLANES, S5_LANES), F32),
            pltpu.VMEM((rows, S5_LANES), F32),
            pltpu.VMEM((rows, S5_LANES), F32),
        ],
        compiler_params=pltpu.CompilerParams(
            dimension_semantics=("arbitrary",), vmem_limit_bytes=VMEM_LIMIT),
        name="s5_mixer",
    )(u, lam, bblk, cblk, dvec, wglu, w1.reshape(N_EXPERTS * D_MODEL, 2 * D_FF), perm,
      w2.reshape(N_EXPERTS * D_FF, D_MODEL))
    return (y, w1p.reshape(N_EXPERTS, D_MODEL, 2 * D_FF), w2b.reshape(N_EXPERTS, D_FF, D_MODEL))


def _ret_body(q_ref, k_ref, v_ref, g_ref, intra_ref, qd_ref, kd_ref, cd_ref, gn_ref, o_ref,
              st_ref):
    tr = q_ref.shape[1]

    @pl.when(pl.program_id(1) == 0)
    def _init():
        st_ref[...] = jnp.zeros_like(st_ref)

    for h in range(RET_HEADS):
        qs = slice(h * RET_DK, (h + 1) * RET_DK)
        vs = slice(h * RET_DV, (h + 1) * RET_DV)
        for c in range(tr // RET_CHUNK):
            rs = slice(c * RET_CHUNK, (c + 1) * RET_CHUNK)
            qc = q_ref[0, rs, qs]
            kc = k_ref[0, rs, qs]
            vc = v_ref[0, rs, vs]
            scores = lax.dot_general(qc, kc, (((1,), (1,)), ((), ())),
                                     preferred_element_type=F32) * intra_ref[h]
            inner = jnp.dot(scores.astype(BF16), vc, preferred_element_type=F32)
            st = st_ref[h]
            qdec = (qc.astype(F32) * qd_ref[h]).astype(BF16)
            cross = jnp.dot(qdec, st.astype(BF16), preferred_element_type=F32)
            kdec_t = (kc.astype(F32) * kd_ref[h]).T.astype(BF16)
            st_ref[h] = cd_ref[h] * st + jnp.dot(kdec_t, vc, preferred_element_type=F32)
            ret = inner + cross
            mu = jnp.mean(ret, axis=-1, keepdims=True)
            dev = ret - mu
            var = jnp.mean(dev * dev, axis=-1, keepdims=True)
            yn = dev * lax.rsqrt(var + NORM_EPS) * gn_ref[0:1, vs]
            o_ref[0, rs, vs] = (jax.nn.silu(g_ref[0, rs, vs].astype(F32)) * yn).astype(BF16)


def _retention(q, k, v, g, intra, qd, kd, cd, gn):
    bsz, seqlen, _ = q.shape
    tr = min(TR_RET, seqlen)
    row = lambda b, t: (b, t, 0)
    const3 = lambda b, t: (0, 0, 0)
    return pl.pallas_call(
        _ret_body,
        grid=(bsz, seqlen // tr),
        in_specs=[
            pl.BlockSpec((1, tr, RET_QK_WIDTH), row),
            pl.BlockSpec((1, tr, RET_QK_WIDTH), row),
            pl.BlockSpec((1, tr, RET_V_WIDTH), row),
            pl.BlockSpec((1, tr, RET_V_WIDTH), row),
            pl.BlockSpec((RET_HEADS, RET_CHUNK, RET_CHUNK), const3),
            pl.BlockSpec((RET_HEADS, RET_CHUNK, RET_DK), const3),
            pl.BlockSpec((RET_HEADS, RET_CHUNK, RET_DK), const3),
            pl.BlockSpec((RET_HEADS, 1, RET_DV), const3),
            pl.BlockSpec((1, RET_V_WIDTH), lambda b, t: (0, 0)),
        ],
        out_specs=pl.BlockSpec((1, tr, RET_V_WIDTH), row),
        out_shape=jax.ShapeDtypeStruct((bsz, seqlen, RET_V_WIDTH), BF16),
        scratch_shapes=[pltpu.VMEM((RET_HEADS, RET_DK, RET_DV), F32)],
        compiler_params=pltpu.CompilerParams(
            dimension_semantics=("arbitrary", "arbitrary"), vmem_limit_bytes=VMEM_LIMIT),
        name="retention",
    )(q, k, v, g, intra, qd, kd, cd, gn)


def _merge_body(x_ref, ys_ref, rg_ref, ga_ref, gb_ref, wbs_ref, wbr_ref, wo_ref, gf_ref,
                rw_ref, rb_ref,
                x1_ref, ri_ref, gt_ref, cnt_ref, xrows_ref,
                run_ref, hrows_ref, dvm_ref, dsm_ref, sem, isem):
    tm = x_ref.shape[1]
    cap = (xrows_ref.shape[0] - TOP_K * tm) // N_EXPERTS
    step = pl.program_id(0) * pl.num_programs(1) + pl.program_id(1)
    last = pl.num_programs(0) * pl.num_programs(1) - 1
    slot = step % 2

    def dests_to_smem():
        return pltpu.make_async_copy(dvm_ref, dsm_ref, isem)

    @pl.when(step == 0)
    def _init():
        run_ref[...] = jnp.zeros_like(run_ref)
        hrows_ref[1] = jnp.zeros((tm, SUBLANES, LANES), F32)
        k_i = lax.broadcasted_iota(jnp.int32, (SUBLANES, tm), 0)
        r_j = lax.broadcasted_iota(jnp.int32, (SUBLANES, tm), 1)
        dvm_ref[...] = N_EXPERTS * cap + TOP_K * r_j + jnp.minimum(k_i, TOP_K - 1)
        dests_to_smem().start()

    y_a = jnp.dot(ys_ref[...], wbs_ref[...], preferred_element_type=F32)
    y_b = jnp.dot(rg_ref[0], wbr_ref[...], preferred_element_type=F32)

    dests_to_smem().wait()
    pending = hrows_ref.at[1 - slot]
    for r in range(tm):
        for kk in range(TOP_K):
            pltpu.make_async_copy(pending.at[r], xrows_ref.at[dsm_ref[kk, r]],
                                  sem).start(priority=kk % 2)

    merged = _sigmoid(ga_ref[0].astype(F32)) * y_a + _sigmoid(gb_ref[0].astype(F32)) * y_b
    x1 = x_ref[0] + jnp.dot(merged.astype(BF16), wo_ref[...], preferred_element_type=F32)
    x1_ref[0] = x1
    h2 = _rms(x1, gf_ref[...])

    lane = lax.broadcasted_iota(jnp.int32, (tm, LANES), 1)
    h_hi = h2.astype(BF16)
    h_lo = (h2 - h_hi.astype(F32)).astype(BF16)
    hi_terms = jnp.dot(h_hi, rw_ref[...], preferred_element_type=F32)
    logits = (hi_terms[:, :LANES] + hi_terms[:, LANES:]
              + jnp.dot(h_lo, rw_ref[:, :LANES], preferred_element_type=F32)) + rb_ref[...]
    work = jnp.where(lane < N_EXPERTS, logits, -jnp.inf)
    lane_f = lane.astype(F32)
    vals, hots, idxs = [], [], []
    for kk in range(TOP_K):
        m = jnp.max(work, axis=-1, keepdims=True)
        idx = jnp.min(jnp.where(work == m, lane_f, float(LANES)), axis=-1, keepdims=True)
        hot = lane_f == idx
        vals.append(m)
        hots.append(hot)
        idxs.append(idx)
        work = jnp.where(hot, -jnp.inf, work)
    exps = [jnp.exp(v - vals[0]) for v in vals]
    denom = exps[0] + exps[1] + exps[2] + exps[3]
    gt = jnp.zeros((tm, LANES), F32)
    for kk in range(TOP_K):
        gt = jnp.where(lane == kk, exps[kk] / denom, gt)
    gt_ref[0] = gt[:, :2 * TOP_K]

    onehot = [h.astype(F32) for h in hots]
    tot = onehot[0] + onehot[1] + onehot[2] + onehot[3]
    r_i = lax.broadcasted_iota(jnp.int32, (tm, tm), 0)
    c_i = lax.broadcasted_iota(jnp.int32, (tm, tm), 1)
    tril = (c_i < r_i).astype(BF16)
    base = jnp.dot(tril, tot.astype(BF16), preferred_element_type=F32) + run_ref[...]
    dmat = jnp.zeros((tm, LANES), F32)
    for kk in range(TOP_K):
        rank = jnp.sum(onehot[kk] * base, axis=-1, keepdims=True)
        dmat = jnp.where(lane == kk, idxs[kk] * float(cap) + rank, dmat)
    ri_ref[0] = dmat.astype(jnp.int32)[:, :2 * TOP_K]
    run_ref[...] = run_ref[...] + jnp.sum(tot, axis=0, keepdims=True)
    cnt_ref[...] = run_ref[...]

    hrows_ref[slot] = h2.reshape(tm, SUBLANES, LANES)

    def drain():
        def wait(r, c):
            for kk in range(TOP_K):
                pltpu.make_async_copy(hrows_ref.at[0, 0], xrows_ref.at[0], sem).wait()
            return c
        lax.fori_loop(0, tm, wait, 0, unroll=8)

    drain()
    dvm_ref[...] = dmat.T[:SUBLANES, :].astype(jnp.int32)
    dests_to_smem().start()

    @pl.when(step == last)
    def _flush():
        dests_to_smem().wait()
        mine = hrows_ref.at[slot]

        def start(r, c):
            for kk in range(TOP_K):
                pltpu.make_async_copy(mine.at[r], xrows_ref.at[dsm_ref[kk, r]],
                                      sem).start(priority=kk % 2)
            return c

        lax.fori_loop(0, tm, start, 0, unroll=8)
        drain()


def _merge(x, ys5, retg, ga, gb, wbs, wbr, wo, gf, rw, rb):
    bsz, seqlen, _ = x.shape
    n_tok = bsz * seqlen
    tm = min(TM_PROJ, seqlen)
    cap = n_tok
    row = lambda b, t: (b, t, 0)
    const = lambda b, t: (0, 0)
    out_shape = (
        jax.ShapeDtypeStruct((bsz, seqlen, D_MODEL), F32),
        jax.ShapeDtypeStruct((bsz, seqlen, 2 * TOP_K), jnp.int32),
        jax.ShapeDtypeStruct((bsz, seqlen, 2 * TOP_K), F32),
        jax.ShapeDtypeStruct((1, LANES), F32),
        jax.ShapeDtypeStruct((N_EXPERTS * cap + TOP_K * tm, SUBLANES, LANES), F32),
    )
    return pl.pallas_call(
        _merge_body,
        grid=(bsz, seqlen // tm),
        in_specs=[
            pl.BlockSpec((1, tm, D_MODEL), row),
            pl.BlockSpec((tm, S5_WIDTH), lambda b, t: (t, b)),
            pl.BlockSpec((1, tm, RET_V_WIDTH), row),
            pl.BlockSpec((1, tm, D_MODEL), row),
            pl.BlockSpec((1, tm, D_MODEL), row),
            pl.BlockSpec((S5_WIDTH, D_MODEL), const),
            pl.BlockSpec((RET_V_WIDTH, D_MODEL), const),
            pl.BlockSpec((D_MODEL, D_MODEL), const),
            pl.BlockSpec((1, D_MODEL), const),
            pl.BlockSpec((D_MODEL, 2 * LANES), const),
            pl.BlockSpec((1, LANES), const),
        ],
        out_specs=(
            pl.BlockSpec((1, tm, D_MODEL), row),
            pl.BlockSpec((1, tm, 2 * TOP_K), row),
            pl.BlockSpec((1, tm, 2 * TOP_K), row),
            pl.BlockSpec((1, LANES), const),
            pl.BlockSpec(memory_space=pl.ANY),
        ),
        out_shape=out_shape,
        scratch_shapes=[
            pltpu.VMEM((1, LANES), F32),
            pltpu.VMEM((2, tm, SUBLANES, LANES), F32),
            pltpu.VMEM((SUBLANES, tm), jnp.int32),
            pltpu.SMEM((SUBLANES, tm), jnp.int32),
            pltpu.SemaphoreType.DMA(()),
            pltpu.SemaphoreType.DMA(()),
        ],
        compiler_params=pltpu.CompilerParams(
            dimension_semantics=("arbitrary", "arbitrary"), has_side_effects=True,
            vmem_limit_bytes=VMEM_LIMIT),
        name="merge_router_dispatch",
    )(x, ys5, retg, ga, gb, wbs, wbr, wo, gf, rw, rb)


def _expert_body(be_ref, brow_ref, bvalid_ref, na_ref, x_ref, w1_ref, b1_ref, w2_ref, b2_ref,
                 y_ref):
    del be_ref, brow_ref
    i = pl.program_id(0)

    @pl.when(i < na_ref[0])
    def _compute():
        bm = x_ref.shape[0]
        row = lax.broadcasted_iota(jnp.int32, (bm, D_MODEL), 0)
        xb = jnp.where(row < bvalid_ref[i], x_ref[...].reshape(bm, D_MODEL), 0.0).astype(BF16)
        gu = jnp.dot(xb, w1_ref[0], preferred_element_type=F32) + b1_ref[0]
        x_glu = jnp.minimum(gu[:, :D_FF], SWIGLU_LIMIT)
        x_lin = jnp.clip(gu[:, D_FF:], -SWIGLU_LIMIT, SWIGLU_LIMIT)
        act = x_glu * jax.nn.sigmoid(SWIGLU_ALPHA * x_glu) * (x_lin + 1.0)
        y = jnp.dot(act.astype(BF16), w2_ref[0], preferred_element_type=F32) + b2_ref[0]
        y_ref[...] = y.reshape(bm, SUBLANES, LANES)


def _experts(block_expert, block_row, block_valid, n_active, x_rows, w1, b1, w2, b2, bm):
    n_blocks = block_expert.shape[0]
    xmap = lambda i, be, br, bv, na: (br[i], 0, 0)
    wmap = lambda i, be, br, bv, na: (be[i], 0, 0)
    grid_spec = pltpu.PrefetchScalarGridSpec(
        num_scalar_prefetch=4,
        grid=(n_blocks,),
        in_specs=[
            pl.BlockSpec((bm, SUBLANES, LANES), xmap),
            pl.BlockSpec((1, D_MODEL, 2 * D_FF), wmap),
            pl.BlockSpec((1, 1, 2 * D_FF), wmap),
            pl.BlockSpec((1, D_FF, D_MODEL), wmap),
            pl.BlockSpec((1, 1, D_MODEL), wmap),
        ],
        out_specs=pl.BlockSpec((bm, SUBLANES, LANES), xmap),
    )
    return pl.pallas_call(
        _expert_body,
        grid_spec=grid_spec,
        out_shape=jax.ShapeDtypeStruct(x_rows.shape, F32),
        compiler_params=pltpu.CompilerParams(
            dimension_semantics=("arbitrary",), vmem_limit_bytes=VMEM_LIMIT),
        name="expert_ffn",
    )(block_expert, block_row, block_valid, n_active, x_rows, w1, b1, w2, b2)


def _combine_body(dest_ref, nxt_ref, gt_ref, x1_ref, gfin_ref, yrows_ref, o_ref, ybuf_ref, sems):
    td = x1_ref.shape[0]
    step = pl.program_id(0)
    slot = step % 2

    def issue(idx_ref, which):
        def start(r, c):
            for kk in range(TOP_K):
                d = idx_ref[0, 0, r * TOP_K + kk]
                pltpu.make_async_copy(yrows_ref.at[d], ybuf_ref.at[which, kk, r],
                                      sems.at[which]).start(priority=kk % 2)
            return c
        lax.fori_loop(0, td, start, 0, unroll=8)

    @pl.when(step == 0)
    def _first():
        issue(dest_ref, 0)

    for par in range(2):
        @pl.when((step < pl.num_programs(0) - 1) & (slot == par))
        def _prefetch(par=par):
            issue(nxt_ref, 1 - par)

    def wait(r, c):
        for kk in range(TOP_K):
            pltpu.make_async_copy(yrows_ref.at[0], ybuf_ref.at[0, kk, 0], sems.at[slot]).wait()
        return c

    lax.fori_loop(0, td, wait, 0, unroll=8)
    gt = gt_ref[...]
    acc = x1_ref[...]
    for kk in range(TOP_K):
        acc = acc + gt[:, kk:kk + 1] * ybuf_ref[slot, kk].reshape(td, D_MODEL)
    o_ref[...] = _rms(acc, gfin_ref[...])


def _combine(dest, gates, x1, gfin, y_rows):
    n_tok = x1.shape[0]
    td = min(TD_ROWS, n_tok)
    n_steps = n_tok // td
    dest3 = dest.reshape(n_steps, 1, td * TOP_K)
    return pl.pallas_call(
        _combine_body,
        grid=(n_steps,),
        in_specs=[
            pl.BlockSpec((1, 1, td * TOP_K), lambda i: (i, 0, 0), memory_space=pltpu.SMEM),
            pl.BlockSpec((1, 1, td * TOP_K), lambda i: (jnp.minimum(i + 1, n_steps - 1), 0, 0),
                         memory_space=pltpu.SMEM),
            pl.BlockSpec((td, 2 * TOP_K), lambda i: (i, 0)),
            pl.BlockSpec((td, D_MODEL), lambda i: (i, 0)),
            pl.BlockSpec((1, D_MODEL), lambda i: (0, 0)),
            pl.BlockSpec(memory_space=pl.ANY),
        ],
        out_specs=pl.BlockSpec((td, D_MODEL), lambda i: (i, 0)),
        out_shape=jax.ShapeDtypeStruct((n_tok, D_MODEL), F32),
        scratch_shapes=[pltpu.VMEM((2, TOP_K, td, SUBLANES, LANES), F32),
                        pltpu.SemaphoreType.DMA((2,))],
        compiler_params=pltpu.CompilerParams(
            dimension_semantics=("arbitrary",), vmem_limit_bytes=VMEM_LIMIT),
        name="combine",
    )(dest3, dest3, gates, x1, gfin, y_rows)


def _block_diag(p):
    g, a, b = p.shape
    eye = jnp.eye(g, dtype=p.dtype)
    return (p[:, :, None, :] * eye[:, None, :, None]).reshape(g * a, g * b)


def _retention_tables():
    log_g = jnp.log1p(-(2.0 ** (-5.0 - jnp.arange(RET_HEADS, dtype=F32))))
    pos = jnp.arange(RET_CHUNK, dtype=F32)
    rel = pos[:, None] - pos[None, :]
    intra = jnp.where(rel >= 0, jnp.exp(log_g[:, None, None] * jnp.maximum(rel, 0.0)), 0.0)
    q_decay = jnp.exp(log_g[:, None] * (pos + 1.0))
    k_decay = jnp.exp(log_g[:, None] * (RET_CHUNK - 1.0 - pos))
    chunk_decay = jnp.exp(log_g * RET_CHUNK)
    qd = jnp.broadcast_to(q_decay[:, :, None], (RET_HEADS, RET_CHUNK, RET_DK))
    kd = jnp.broadcast_to(k_decay[:, :, None], (RET_HEADS, RET_CHUNK, RET_DK))
    cd = jnp.broadcast_to(chunk_decay[:, None, None], (RET_HEADS, 1, RET_DV))
    return intra, qd, kd, cd


def _layer(x, positions, norm_mix_g, w_in, lam_re, lam_im, log_dt, b_re, b_im, c_re, c_im,
           s5_d, w_glu, b_glu, ret_gn_g, w_branch_s5, w_branch_ret, w_out, norm_ffn_g,
           router_w, router_b, w1, b1, w2, b2, g_out):
    bsz, seqlen, _ = x.shape
    n_tok = bsz * seqlen

    half = RET_DK // 2
    inv_freq = ROPE_BASE ** (-jnp.arange(half, dtype=F32) / half)
    rope = jnp.stack([jnp.concatenate([inv_freq, inv_freq]),
                      jnp.concatenate([-jnp.ones((half,), F32), jnp.ones((half,), F32)])])
    pos128 = jnp.broadcast_to(positions.astype(F32)[..., None], (bsz, seqlen, LANES))
    u, q, k, v, g_ret, gate_a, gate_b = _inproj(
        x, pos128, rope, norm_mix_g.reshape(1, D_MODEL), w_in.astype(BF16))

    lam = jnp.stack([lam_re.reshape(-1), lam_im.reshape(-1),
                     jnp.repeat(log_dt, S5_STATE)]).astype(F32)
    gpc = S5_GROUPS // S5_CHUNKS

    def chunked_block_diag(p):
        return jax.vmap(_block_diag)(p.reshape(S5_CHUNKS, gpc, p.shape[1], p.shape[2]))

    bblk = jnp.stack([chunked_block_diag(jnp.swapaxes(b_re, 1, 2)),
                      chunked_block_diag(jnp.swapaxes(b_im, 1, 2))])
    cblk = jnp.stack([chunked_block_diag(jnp.swapaxes(c_re, 1, 2)),
                      chunked_block_diag(jnp.swapaxes(c_im, 1, 2))]).astype(BF16)
    dvec = jnp.stack([s5_d, b_glu])
    ys5, w1p, w2b = _s5(u, lam, bblk, cblk, dvec, w_glu.astype(BF16), bsz, w1, w2)

    intra, qd, kd, cd = _retention_tables()
    retg = _retention(q, k, v, g_ret, intra, qd, kd, cd, ret_gn_g.reshape(1, RET_V_WIDTH))

    rw_hi = router_w.astype(BF16)
    rw_lo = (router_w - rw_hi.astype(F32)).astype(BF16)
    rw = (jnp.zeros((D_MODEL, 2 * LANES), BF16).at[:, :N_EXPERTS].set(rw_hi)
          .at[:, LANES:LANES + N_EXPERTS].set(rw_lo))
    rb = jnp.zeros((1, LANES), F32).at[0, :N_EXPERTS].set(router_b)
    x1, route, gates, counts, x_rows = _merge(
        x, ys5, retg, gate_a, gate_b, w_branch_s5.astype(BF16), w_branch_ret.astype(BF16),
        w_out.astype(BF16), norm_ffn_g.reshape(1, D_MODEL), rw, rb)

    bm = BM_EXPERT
    cap = n_tok
    n_blocks = n_tok * TOP_K // bm + N_EXPERTS
    dest = route.reshape(n_tok, 2 * TOP_K)[:, :TOP_K]
    cnt = counts[0, :N_EXPERTS].astype(jnp.int32)
    nblk = (cnt + bm - 1) // bm
    bend = jnp.cumsum(nblk)
    n_active = bend[N_EXPERTS - 1:]
    blk = jnp.minimum(jnp.arange(n_blocks, dtype=jnp.int32), n_active[0] - 1)
    block_expert = jnp.sum((bend[None, :] <= blk[:, None]).astype(jnp.int32), axis=1)
    sel = (jnp.arange(N_EXPERTS, dtype=jnp.int32)[None, :] == block_expert[:, None]).astype(jnp.int32)
    j_in = blk - jnp.sum(sel * (bend - nblk)[None, :], axis=1)
    block_row = block_expert * (cap // bm) + j_in
    block_valid = jnp.clip(jnp.sum(sel * cnt[None, :], axis=1) - j_in * bm, 0, bm)

    b1p = jnp.concatenate([b1[:, 0::2], b1[:, 1::2]], axis=-1).reshape(N_EXPERTS, 1, 2 * D_FF)
    y_rows = _experts(block_expert.astype(jnp.int32), block_row.astype(jnp.int32),
                      block_valid.astype(jnp.int32), n_active.astype(jnp.int32), x_rows, w1p, b1p,
                      w2b, b2.reshape(N_EXPERTS, 1, D_MODEL), bm)
    out = _combine(dest, gates.reshape(n_tok, 2 * TOP_K), x1.reshape(n_tok, D_MODEL),
                   g_out.reshape(1, D_MODEL), y_rows)
    return out.reshape(bsz, seqlen, D_MODEL)


def kernel(x, positions, norm_mix_g, w_in, s5_lambda_re, s5_lambda_im, s5_log_dt, s5_b_re, s5_b_im, s5_c_re, s5_c_im, s5_d, s5_w_glu, s5_b_glu, ret_gn_g, w_branch_s5, w_branch_ret, w_out, norm_ffn_g, router_w, router_b, expert_w1, expert_b1, expert_w2, expert_b2, norm_final_g):
    assert norm_mix_g.shape[0] == 1, "single-layer trunk"
    return _layer(x, positions, norm_mix_g[0], w_in[0], s5_lambda_re[0], s5_lambda_im[0],
                  s5_log_dt[0], s5_b_re[0], s5_b_im[0], s5_c_re[0], s5_c_im[0], s5_d[0],
                  s5_w_glu[0], s5_b_glu[0], ret_gn_g[0], w_branch_s5[0], w_branch_ret[0],
                  w_out[0], norm_ffn_g[0], router_w[0], router_b[0], expert_w1[0], expert_b1[0],
                  expert_w2[0], expert_b2[0], norm_final_g)
```

```python
import jax
import jax.numpy as jnp
from jax import lax
from jax.experimental import pallas as pl
from jax.experimental.pallas import tpu as pltpu

F32 = jnp.float32
BF16 = jnp.bfloat16

D_MODEL = 1024
S5_WIDTH = 512
S5_GROUP = 16
S5_GROUPS = 32
S5_STATE = 64
S5_LANES = S5_GROUPS * S5_STATE
RET_HEADS = 4
RET_DK = 128
RET_DV = 256
RET_QK_WIDTH = RET_HEADS * RET_DK
RET_V_WIDTH = RET_HEADS * RET_DV
RET_CHUNK = 128
ROPE_BASE = 10000.0
N_EXPERTS = 32
TOP_K = 4
D_FF = 1024
SWIGLU_ALPHA = 1.702
SWIGLU_LIMIT = 7.0
NORM_EPS = 1e-5
IN_WIDTH = S5_WIDTH + 2 * RET_QK_WIDTH + 2 * RET_V_WIDTH + 2 * D_MODEL
OFF_Q = S5_WIDTH
OFF_V = OFF_Q + 2 * RET_QK_WIDTH
OFF_G = OFF_V + RET_V_WIDTH
OFF_GA = OFF_G + RET_V_WIDTH
OFF_GB = OFF_GA + D_MODEL

LANES = 128
SUBLANES = 8
VMEM_LIMIT = 56 * 1024 * 1024

TM_PROJ = 512
TT_S5 = 128
S5_LC = 512
S5_CHUNKS = 4
TR_RET = 512
BM_EXPERT = 512
TD_ROWS = 512


def _rms(x, g):
    return x * lax.rsqrt(jnp.mean(x * x, axis=-1, keepdims=True) + NORM_EPS) * g


def _sigmoid(x):
    return 0.5 * jnp.tanh(0.5 * x) + 0.5


def _inproj_body(x_ref, pos_ref, rope_ref, g_ref, w_ref,
                 u_ref, q_ref, k_ref, v_ref, gr_ref, ga_ref, gb_ref):
    hb = _rms(x_ref[0], g_ref[...]).astype(BF16)

    def proj(lo, width):
        return jnp.dot(hb, w_ref[:, lo:lo + width], preferred_element_type=F32)

    u_ref[...] = proj(0, S5_WIDTH).astype(BF16)
    ang = pos_ref[0] * rope_ref[0:1, :]
    cos = jnp.cos(ang)
    sin = jnp.sin(ang) * rope_ref[1:2, :]
    qk = proj(OFF_Q, 2 * RET_QK_WIDTH)
    for h in range(RET_HEADS):
        qh = qk[:, h * RET_DK:(h + 1) * RET_DK]
        q_ref[0, :, h * RET_DK:(h + 1) * RET_DK] = (
            qh * cos + pltpu.roll(qh, RET_DK // 2, 1) * sin).astype(BF16)
        kh = qk[:, RET_QK_WIDTH + h * RET_DK:RET_QK_WIDTH + (h + 1) * RET_DK]
        k_ref[0, :, h * RET_DK:(h + 1) * RET_DK] = (
            (kh * cos + pltpu.roll(kh, RET_DK // 2, 1) * sin) * (RET_DK ** -0.5)).astype(BF16)
    v_ref[0] = proj(OFF_V, RET_V_WIDTH).astype(BF16)
    gr_ref[0] = proj(OFF_G, RET_V_WIDTH).astype(BF16)
    ga_ref[0] = proj(OFF_GA, D_MODEL).astype(BF16)
    gb_ref[0] = proj(OFF_GB, D_MODEL).astype(BF16)


def _inproj(x, pos128, rope, g, w_bf):
    bsz, seqlen, _ = x.shape
    tm = min(TM_PROJ, seqlen)
    grid = (bsz, seqlen // tm)
    row = lambda b, t: (b, t, 0)
    const = lambda b, t: (0, 0)
    out_shape = (
        jax.ShapeDtypeStruct((seqlen, bsz * S5_WIDTH), BF16),
        jax.ShapeDtypeStruct((bsz, seqlen, RET_QK_WIDTH), BF16),
        jax.ShapeDtypeStruct((bsz, seqlen, RET_QK_WIDTH), BF16),
        jax.ShapeDtypeStruct((bsz, seqlen, RET_V_WIDTH), BF16),
        jax.ShapeDtypeStruct((bsz, seqlen, RET_V_WIDTH), BF16),
        jax.ShapeDtypeStruct((bsz, seqlen, D_MODEL), BF16),
        jax.ShapeDtypeStruct((bsz, seqlen, D_MODEL), BF16),
    )
    return pl.pallas_call(
        _inproj_body,
        grid=grid,
        in_specs=[
            pl.BlockSpec((1, tm, D_MODEL), row),
            pl.BlockSpec((1, tm, LANES), row),
            pl.BlockSpec((2, LANES), const),
            pl.BlockSpec((1, D_MODEL), const),
            pl.BlockSpec((D_MODEL, IN_WIDTH), const, pipeline_mode=pl.Buffered(1)),
        ],
        out_specs=(
            pl.BlockSpec((tm, S5_WIDTH), lambda b, t: (t, b)),
            pl.BlockSpec((1, tm, RET_QK_WIDTH), row),
            pl.BlockSpec((1, tm, RET_QK_WIDTH), row),
            pl.BlockSpec((1, tm, RET_V_WIDTH), row),
            pl.BlockSpec((1, tm, RET_V_WIDTH), row),
            pl.BlockSpec((1, tm, D_MODEL), row),
            pl.BlockSpec((1, tm, D_MODEL), row),
        ),
        out_shape=out_shape,
        compiler_params=pltpu.CompilerParams(
            dimension_semantics=("arbitrary", "arbitrary"), vmem_limit_bytes=VMEM_LIMIT),
        name="inproj",
    )(x, pos128, rope, g, w_bf)


def _prep_expert_weights(w1_ref, perm_ref, w2_ref, w1o_ref, w2o_ref):
    for c in range(D_FF // LANES):
        wc = w1_ref[:, c * 2 * LANES:(c + 1) * 2 * LANES].astype(BF16)
        r = jnp.dot(wc, perm_ref[...], preferred_element_type=F32).astype(BF16)
        w1o_ref[:, c * LANES:(c + 1) * LANES] = r[:, :LANES]
        w1o_ref[:, D_FF + c * LANES:D_FF + (c + 1) * LANES] = r[:, LANES:]
    w2o_ref[...] = w2_ref[...].astype(BF16)


def _s5_body(u_ref, lam_ref, bblk_ref, cblk_ref, dvec_ref, wglu_ref, o_ref,
             bf_ref, a8_ref, p_ref, xre_ref, xim_ref):
    tt = u_ref.shape[0]
    rows = xre_ref.shape[0]
    bsz = rows // tt
    cw = S5_WIDTH // S5_CHUNKS
    cl = S5_LANES // S5_CHUNKS

    @pl.when(pl.program_id(0) == 0)
    def _init():
        lr = lam_ref[0:1, :]
        li = lam_ref[1:2, :]
        dt = jnp.exp(lam_ref[2:3, :])
        mag = jnp.exp(lr * dt)
        a_re = mag * jnp.cos(li * dt)
        a_im = mag * jnp.sin(li * dt)
        nr = a_re - 1.0
        den = lr * lr + li * li
        f_re = (nr * lr + a_im * li) / den
        f_im = (a_im * lr - nr * li) / den
        for c in range(S5_CHUNKS):
            fr = f_re[:, c * cl:(c + 1) * cl]
            fi = f_im[:, c * cl:(c + 1) * cl]
            b_re = bblk_ref[0, c]
            b_im = bblk_ref[1, c]
            bf_ref[0, c] = (b_re * fr - b_im * fi).astype(BF16)
            bf_ref[1, c] = (b_im * fr + b_re * fi).astype(BF16)
        a8_ref[0] = jnp.broadcast_to(a_re, (SUBLANES, S5_LANES))
        a8_ref[1] = jnp.broadcast_to(a_im, (SUBLANES, S5_LANES))
        p_ref[...] = jnp.zeros_like(p_ref)

    r_i = lax.broadcasted_iota(jnp.int32, (rows, tt), 0)
    t_i = lax.broadcasted_iota(jnp.int32, (rows, tt), 1)
    u2 = jnp.zeros((rows, S5_WIDTH), F32)
    for bb in range(bsz):
        spread = (r_i == bsz * t_i + bb).astype(BF16)
        u2 = u2 + jnp.dot(spread, u_ref[:, bb * S5_WIDTH:(bb + 1) * S5_WIDTH],
                          preferred_element_type=F32)
    ub = u2.astype(BF16)
    for c in range(S5_CHUNKS):
        uc = ub[:, c * cw:(c + 1) * cw]
        xre_ref[:, c * cl:(c + 1) * cl] = jnp.dot(uc, bf_ref[0, c], preferred_element_type=F32)
        xim_ref[:, c * cl:(c + 1) * cl] = jnp.dot(uc, bf_ref[1, c], preferred_element_type=F32)

    lower = lax.broadcasted_iota(jnp.int32, (SUBLANES, S5_LC), 0) < (SUBLANES // 2)
    for j in range(S5_LANES // S5_LC):
        sl = slice(j * S5_LC, (j + 1) * S5_LC)
        ar = a8_ref[0, :, sl]
        ai = a8_ref[1, :, sl]

        def step(k, carry, sl=sl, ar=ar, ai=ai):
            pr, pi = carry
            r0 = pl.multiple_of(k * SUBLANES, SUBLANES)
            xr = xre_ref[pl.ds(r0, SUBLANES), sl]
            xi = xim_ref[pl.ds(r0, SUBLANES), sl]
            s1r = ar * pr - ai * pi + xr
            s1i = ar * pi + ai * pr + xi
            tr = pltpu.roll(s1r, SUBLANES // 2, 0)
            ti = pltpu.roll(s1i, SUBLANES // 2, 0)
            s2r = ar * tr - ai * ti + xr
            s2i = ar * ti + ai * tr + xi
            xre_ref[pl.ds(r0, SUBLANES), sl] = jnp.where(lower, s1r, s2r)
            xim_ref[pl.ds(r0, SUBLANES), sl] = jnp.where(lower, s1i, s2i)
            return pltpu.roll(s2r, SUBLANES // 2, 0), pltpu.roll(s2i, SUBLANES // 2, 0)

        pr, pi = lax.fori_loop(0, rows // SUBLANES, step, (p_ref[0, :, sl], p_ref[1, :, sl]),
                               unroll=2)
        p_ref[0, :, sl] = pr
        p_ref[1, :, sl] = pi

    y = jnp.concatenate([
        jnp.dot(xre_ref[:, c * cl:(c + 1) * cl].astype(BF16), cblk_ref[0, c],
                preferred_element_type=F32)
        - jnp.dot(xim_ref[:, c * cl:(c + 1) * cl].astype(BF16), cblk_ref[1, c],
                  preferred_element_type=F32)
        for c in range(S5_CHUNKS)], axis=1)
    y = y + dvec_ref[0:1, :] * u2
    z = jax.nn.gelu(y)
    gate = jax.nn.sigmoid(
        jnp.dot(z.astype(BF16), wglu_ref[...], preferred_element_type=F32) + dvec_ref[1:2, :])
    out = (z * gate).astype(BF16)
    t_o = lax.broadcasted_iota(jnp.int32, (tt, rows), 0)
    r_o = lax.broadcasted_iota(jnp.int32, (tt, rows), 1)
    for bb in range(bsz):
        pick = (r_o == bsz * t_o + bb).astype(BF16)
        o_ref[:, bb * S5_WIDTH:(bb + 1) * S5_WIDTH] = jnp.dot(
            pick, out, preferred_element_type=F32).astype(BF16)


def _s5(u, lam, bblk, cblk, dvec, wglu, bsz):
    seqlen = u.shape[0]
    tt = min(TT_S5, seqlen)
    rows = tt * bsz
    assert seqlen % tt == 0 and rows % (2 * SUBLANES) == 0 and SUBLANES % bsz == 0
    const2 = lambda i: (0, 0)
    const4 = lambda i: (0, 0, 0, 0)
    cw = S5_WIDTH // S5_CHUNKS
    cl = S5_LANES // S5_CHUNKS
    return pl.pallas_call(
        _s5_body,
        grid=(seqlen // tt,),
        in_specs=[
            pl.BlockSpec((tt, bsz * S5_WIDTH), lambda i: (i, 0)),
            pl.BlockSpec((3, S5_LANES), const2),
            pl.BlockSpec((2, S5_CHUNKS, cw, cl), const4),
            pl.BlockSpec((2, S5_CHUNKS, cl, cw), const4),
            pl.BlockSpec((2, S5_WIDTH), const2),
            pl.BlockSpec((S5_WIDTH, S5_WIDTH), const2),
        ],
        out_specs=pl.BlockSpec((tt, bsz * S5_WIDTH), lambda i: (i, 0)),
        out_shape=jax.ShapeDtypeStruct((seqlen, bsz * S5_WIDTH), BF16),
        scratch_shapes=[
            pltpu.VMEM((2, S5_CHUNKS, cw, cl), BF16),
            pltpu.VMEM((2, SUBLANES, S5_LANES), F32),
            pltpu.VMEM((2, SUBLANES, S5_LANES), F32),
            pltpu.VMEM((rows, S5_LANES), F32),
            pltpu.VMEM((rows, S5_LANES), F32),
        ],
        compiler_params=pltpu.CompilerParams(
            dimension_semantics=("arbitrary",), vmem_limit_bytes=VMEM_LIMIT),
        name="s5_mixer",
    )(u, lam, bblk, cblk, dvec, wglu)


def _ret_body(q_ref, k_ref, v_ref, g_ref, intra_ref, qd_ref, kd_ref, cd_ref, gn_ref,
              w1_ref, perm_ref, w2_ref, o_ref, w1o_ref, w2o_ref, st_ref):
    _prep_expert_weights(w1_ref, perm_ref, w2_ref, w1o_ref, w2o_ref)
    tr = q_ref.shape[1]

    @pl.when(pl.program_id(1) == 0)
    def _init():
        st_ref[...] = jnp.zeros_like(st_ref)

    for h in range(RET_HEADS):
        qs = slice(h * RET_DK, (h + 1) * RET_DK)
        vs = slice(h * RET_DV, (h + 1) * RET_DV)
        for c in range(tr // RET_CHUNK):
            rs = slice(c * RET_CHUNK, (c + 1) * RET_CHUNK)
            qc = q_ref[0, rs, qs]
            kc = k_ref[0, rs, qs]
            vc = v_ref[0, rs, vs]
            scores = lax.dot_general(qc, kc, (((1,), (1,)), ((), ())),
                                     preferred_element_type=F32) * intra_ref[h]
            inner = jnp.dot(scores.astype(BF16), vc, preferred_element_type=F32)
            st = st_ref[h]
            qdec = (qc.astype(F32) * qd_ref[h]).astype(BF16)
            cross = jnp.dot(qdec, st.astype(BF16), preferred_element_type=F32)
            kdec_t = (kc.astype(F32) * kd_ref[h]).T.astype(BF16)
            st_ref[h] = cd_ref[h] * st + jnp.dot(kdec_t, vc, preferred_element_type=F32)
            ret = inner + cross
            mu = jnp.mean(ret, axis=-1, keepdims=True)
            dev = ret - mu
            var = jnp.mean(dev * dev, axis=-1, keepdims=True)
            yn = dev * lax.rsqrt(var + NORM_EPS) * gn_ref[0:1, vs]
            o_ref[0, rs, vs] = (jax.nn.silu(g_ref[0, rs, vs].astype(F32)) * yn).astype(BF16)


def _retention(q, k, v, g, intra, qd, kd, cd, gn, w1, w2):
    bsz, seqlen, _ = q.shape
    tr = min(TR_RET, seqlen)
    nt = seqlen // tr
    kr = N_EXPERTS * D_MODEL // (bsz * nt)
    assert kr * bsz * nt == N_EXPERTS * D_MODEL and kr % (2 * SUBLANES) == 0
    row = lambda b, t: (b, t, 0)
    const3 = lambda b, t: (0, 0, 0)
    slab = lambda b, t: (b * nt + t, 0)
    i_p = lax.broadcasted_iota(jnp.int32, (2 * LANES, 2 * LANES), 0)
    j_p = lax.broadcasted_iota(jnp.int32, (2 * LANES, 2 * LANES), 1)
    perm = (i_p == jnp.where(j_p < LANES, 2 * j_p, 2 * (j_p - LANES) + 1)).astype(BF16)
    out, w1p, w2b = pl.pallas_call(
        _ret_body,
        grid=(bsz, nt),
        in_specs=[
            pl.BlockSpec((1, tr, RET_QK_WIDTH), row),
            pl.BlockSpec((1, tr, RET_QK_WIDTH), row),
            pl.BlockSpec((1, tr, RET_V_WIDTH), row),
            pl.BlockSpec((1, tr, RET_V_WIDTH), row),
            pl.BlockSpec((RET_HEADS, RET_CHUNK, RET_CHUNK), const3),
            pl.BlockSpec((RET_HEADS, RET_CHUNK, RET_DK), const3),
            pl.BlockSpec((RET_HEADS, RET_CHUNK, RET_DK), const3),
            pl.BlockSpec((RET_HEADS, 1, RET_DV), const3),
            pl.BlockSpec((1, RET_V_WIDTH), lambda b, t: (0, 0)),
            pl.BlockSpec((kr, 2 * D_FF), slab),
            pl.BlockSpec((2 * LANES, 2 * LANES), lambda b, t: (0, 0)),
            pl.BlockSpec((kr, D_MODEL), slab),
        ],
        out_specs=(
            pl.BlockSpec((1, tr, RET_V_WIDTH), row),
            pl.BlockSpec((kr, 2 * D_FF), slab),
            pl.BlockSpec((kr, D_MODEL), slab),
        ),
        out_shape=(
            jax.ShapeDtypeStruct((bsz, seqlen, RET_V_WIDTH), BF16),
            jax.ShapeDtypeStruct((N_EXPERTS * D_MODEL, 2 * D_FF), BF16),
            jax.ShapeDtypeStruct((N_EXPERTS * D_FF, D_MODEL), BF16),
        ),
        scratch_shapes=[pltpu.VMEM((RET_HEADS, RET_DK, RET_DV), F32)],
        compiler_params=pltpu.CompilerParams(
            dimension_semantics=("arbitrary", "arbitrary"), vmem_limit_bytes=VMEM_LIMIT),
        name="retention",
    )(q, k, v, g, intra, qd, kd, cd, gn, w1.reshape(N_EXPERTS * D_MODEL, 2 * D_FF), perm,
      w2.reshape(N_EXPERTS * D_FF, D_MODEL))
    return (out, w1p.reshape(N_EXPERTS, D_MODEL, 2 * D_FF), w2b.reshape(N_EXPERTS, D_FF, D_MODEL))


def _merge_body(x_ref, ys_ref, rg_ref, ga_ref, gb_ref, wbs_ref, wbr_ref, wo_ref, gf_ref,
                rw_ref, rb_ref,
                x1_ref, ri_ref, gt_ref, cnt_ref, xrows_ref,
                run_ref, hrows_ref, dvm_ref, dsm_ref, sem, isem):
    tm = x_ref.shape[1]
    cap = (xrows_ref.shape[0] - TOP_K * tm) // N_EXPERTS
    step = pl.program_id(0) * pl.num_programs(1) + pl.program_id(1)
    last = pl.num_programs(0) * pl.num_programs(1) - 1
    slot = step % 2

    def dests_to_smem():
        return pltpu.make_async_copy(dvm_ref, dsm_ref, isem)

    @pl.when(step == 0)
    def _init():
        run_ref[...] = jnp.zeros_like(run_ref)
        hrows_ref[1] = jnp.zeros((tm, SUBLANES, LANES), F32)
        k_i = lax.broadcasted_iota(jnp.int32, (SUBLANES, tm), 0)
        r_j = lax.broadcasted_iota(jnp.int32, (SUBLANES, tm), 1)
        dvm_ref[...] = N_EXPERTS * cap + TOP_K * r_j + jnp.minimum(k_i, TOP_K - 1)
        dests_to_smem().start()

    y_a = jnp.dot(ys_ref[...], wbs_ref[...], preferred_element_type=F32)
    y_b = jnp.dot(rg_ref[0], wbr_ref[...], preferred_element_type=F32)

    dests_to_smem().wait()
    pending = hrows_ref.at[1 - slot]
    for r in range(tm):
        for kk in range(TOP_K):
            pltpu.make_async_copy(pending.at[r], xrows_ref.at[dsm_ref[kk, r]],
                                  sem).start(priority=kk % 2)

    merged = _sigmoid(ga_ref[0].astype(F32)) * y_a + _sigmoid(gb_ref[0].astype(F32)) * y_b
    x1 = x_ref[0] + jnp.dot(merged.astype(BF16), wo_ref[...], preferred_element_type=F32)
    x1_ref[0] = x1
    h2 = _rms(x1, gf_ref[...])

    lane = lax.broadcasted_iota(jnp.int32, (tm, LANES), 1)
    h_hi = h2.astype(BF16)
    h_lo = (h2 - h_hi.astype(F32)).astype(BF16)
    hi_terms = jnp.dot(h_hi, rw_ref[...], preferred_element_type=F32)
    logits = (hi_terms[:, :LANES] + hi_terms[:, LANES:]
              + jnp.dot(h_lo, rw_ref[:, :LANES], preferred_element_type=F32)) + rb_ref[...]
    work = jnp.where(lane < N_EXPERTS, logits, -jnp.inf)
    lane_f = lane.astype(F32)
    vals, hots, idxs = [], [], []
    for kk in range(TOP_K):
        m = jnp.max(work, axis=-1, keepdims=True)
        idx = jnp.min(jnp.where(work == m, lane_f, float(LANES)), axis=-1, keepdims=True)
        hot = lane_f == idx
        vals.append(m)
        hots.append(hot)
        idxs.append(idx)
        work = jnp.where(hot, -jnp.inf, work)
    exps = [jnp.exp(v - vals[0]) for v in vals]
    denom = exps[0] + exps[1] + exps[2] + exps[3]
    gt = jnp.zeros((tm, LANES), F32)
    for kk in range(TOP_K):
        gt = jnp.where(lane == kk, exps[kk] / denom, gt)
    gt_ref[0] = gt[:, :2 * TOP_K]

    onehot = [h.astype(F32) for h in hots]
    tot = onehot[0] + onehot[1] + onehot[2] + onehot[3]
    r_i = lax.broadcasted_iota(jnp.int32, (tm, tm), 0)
    c_i = lax.broadcasted_iota(jnp.int32, (tm, tm), 1)
    tril = (c_i < r_i).astype(BF16)
    base = jnp.dot(tril, tot.astype(BF16), preferred_element_type=F32) + run_ref[...]
    dmat = jnp.zeros((tm, LANES), F32)
    for kk in range(TOP_K):
        rank = jnp.sum(onehot[kk] * base, axis=-1, keepdims=True)
        dmat = jnp.where(lane == kk, idxs[kk] * float(cap) + rank, dmat)
    ri_ref[0] = dmat.astype(jnp.int32)[:, :2 * TOP_K]
    run_ref[...] = run_ref[...] + jnp.sum(tot, axis=0, keepdims=True)
    cnt_ref[...] = run_ref[...]

    hrows_ref[slot] = h2.reshape(tm, SUBLANES, LANES)

    def drain():
        def wait(r, c):
            for kk in range(TOP_K):
                pltpu.make_async_copy(hrows_ref.at[0, 0], xrows_ref.at[0], sem).wait()
            return c
        lax.fori_loop(0, tm, wait, 0, unroll=8)

    drain()
    dvm_ref[...] = dmat.T[:SUBLANES, :].astype(jnp.int32)
    dests_to_smem().start()

    @pl.when(step == last)
    def _flush():
        dests_to_smem().wait()
        mine = hrows_ref.at[slot]

        def start(r, c):
            for kk in range(TOP_K):
                pltpu.make_async_copy(mine.at[r], xrows_ref.at[dsm_ref[kk, r]],
                                      sem).start(priority=kk % 2)
            return c

        lax.fori_loop(0, tm, start, 0, unroll=8)
        drain()


def _merge(x, ys5, retg, ga, gb, wbs, wbr, wo, gf, rw, rb):
    bsz, seqlen, _ = x.shape
    n_tok = bsz * seqlen
    tm = min(TM_PROJ, seqlen)
    cap = n_tok
    row = lambda b, t: (b, t, 0)
    const = lambda b, t: (0, 0)
    out_shape = (
        jax.ShapeDtypeStruct((bsz, seqlen, D_MODEL), F32),
        jax.ShapeDtypeStruct((bsz, seqlen, 2 * TOP_K), jnp.int32),
        jax.ShapeDtypeStruct((bsz, seqlen, 2 * TOP_K), F32),
        jax.ShapeDtypeStruct((1, LANES), F32),
        jax.ShapeDtypeStruct((N_EXPERTS * cap + TOP_K * tm, SUBLANES, LANES), F32),
    )
    return pl.pallas_call(
        _merge_body,
        grid=(bsz, seqlen // tm),
        in_specs=[
            pl.BlockSpec((1, tm, D_MODEL), row),
            pl.BlockSpec((tm, S5_WIDTH), lambda b, t: (t, b)),
            pl.BlockSpec((1, tm, RET_V_WIDTH), row),
            pl.BlockSpec((1, tm, D_MODEL), row),
            pl.BlockSpec((1, tm, D_MODEL), row),
            pl.BlockSpec((S5_WIDTH, D_MODEL), const),
            pl.BlockSpec((RET_V_WIDTH, D_MODEL), const),
            pl.BlockSpec((D_MODEL, D_MODEL), const),
            pl.BlockSpec((1, D_MODEL), const),
            pl.BlockSpec((D_MODEL, 2 * LANES), const),
            pl.BlockSpec((1, LANES), const),
        ],
        out_specs=(
            pl.BlockSpec((1, tm, D_MODEL), row),
            pl.BlockSpec((1, tm, 2 * TOP_K), row),
            pl.BlockSpec((1, tm, 2 * TOP_K), row),
            pl.BlockSpec((1, LANES), const),
            pl.BlockSpec(memory_space=pl.ANY),
        ),
        out_shape=out_shape,
        scratch_shapes=[
            pltpu.VMEM((1, LANES), F32),
            pltpu.VMEM((2, tm, SUBLANES, LANES), F32),
            pltpu.VMEM((SUBLANES, tm), jnp.int32),
            pltpu.SMEM((SUBLANES, tm), jnp.int32),
            pltpu.SemaphoreType.DMA(()),
            pltpu.SemaphoreType.DMA(()),
        ],
        compiler_params=pltpu.CompilerParams(
            dimension_semantics=("arbitrary", "arbitrary"), has_side_effects=True,
            vmem_limit_bytes=VMEM_LIMIT),
        name="merge_router_dispatch",
    )(x, ys5, retg, ga, gb, wbs, wbr, wo, gf, rw, rb)


def _expert_body(be_ref, brow_ref, bvalid_ref, na_ref, x_ref, w1_ref, b1_ref, w2_ref, b2_ref,
                 y_ref):
    del be_ref, brow_ref
    i = pl.program_id(0)

    @pl.when(i < na_ref[0])
    def _compute():
        bm = x_ref.shape[0]
        row = lax.broadcasted_iota(jnp.int32, (bm, D_MODEL), 0)
        xb = jnp.where(row < bvalid_ref[i], x_ref[...].reshape(bm, D_MODEL), 0.0).astype(BF16)
        gu = jnp.dot(xb, w1_ref[0], preferred_element_type=F32) + b1_ref[0]
        x_glu = jnp.minimum(gu[:, :D_FF], SWIGLU_LIMIT)
        x_lin = jnp.clip(gu[:, D_FF:], -SWIGLU_LIMIT, SWIGLU_LIMIT)
        act = x_glu * jax.nn.sigmoid(SWIGLU_ALPHA * x_glu) * (x_lin + 1.0)
        y = jnp.dot(act.astype(BF16), w2_ref[0], preferred_element_type=F32) + b2_ref[0]
        y_ref[...] = y.reshape(bm, SUBLANES, LANES)


def _experts(block_expert, block_row, block_valid, n_active, x_rows, w1, b1, w2, b2, bm):
    n_blocks = block_expert.shape[0]
    xmap = lambda i, be, br, bv, na: (br[i], 0, 0)
    wmap = lambda i, be, br, bv, na: (be[i], 0, 0)
    grid_spec = pltpu.PrefetchScalarGridSpec(
        num_scalar_prefetch=4,
        grid=(n_blocks,),
        in_specs=[
            pl.BlockSpec((bm, SUBLANES, LANES), xmap),
            pl.BlockSpec((1, D_MODEL, 2 * D_FF), wmap),
            pl.BlockSpec((1, 1, 2 * D_FF), wmap),
            pl.BlockSpec((1, D_FF, D_MODEL), wmap),
            pl.BlockSpec((1, 1, D_MODEL), wmap),
        ],
        out_specs=pl.BlockSpec((bm, SUBLANES, LANES), xmap),
    )
    return pl.pallas_call(
        _expert_body,
        grid_spec=grid_spec,
        out_shape=jax.ShapeDtypeStruct(x_rows.shape, F32),
        compiler_params=pltpu.CompilerParams(
            dimension_semantics=("arbitrary",), vmem_limit_bytes=VMEM_LIMIT),
        name="expert_ffn",
    )(block_expert, block_row, block_valid, n_active, x_rows, w1, b1, w2, b2)


def _combine_body(dest_ref, nxt_ref, gt_ref, x1_ref, gfin_ref, yrows_ref, o_ref, ybuf_ref, sems):
    td = x1_ref.shape[0]
    step = pl.program_id(0)
    slot = step % 2

    def issue(idx_ref, which):
        def start(r, c):
            for kk in range(TOP_K):
                d = idx_ref[0, 0, r * TOP_K + kk]
                pltpu.make_async_copy(yrows_ref.at[d], ybuf_ref.at[which, kk, r],
                                      sems.at[which]).start(priority=kk % 2)
            return c
        lax.fori_loop(0, td, start, 0, unroll=8)

    @pl.when(step == 0)
    def _first():
        issue(dest_ref, 0)

    for par in range(2):
        @pl.when((step < pl.num_programs(0) - 1) & (slot == par))
        def _prefetch(par=par):
            issue(nxt_ref, 1 - par)

    def wait(r, c):
        for kk in range(TOP_K):
            pltpu.make_async_copy(yrows_ref.at[0], ybuf_ref.at[0, kk, 0], sems.at[slot]).wait()
        return c

    lax.fori_loop(0, td, wait, 0, unroll=8)
    gt = gt_ref[...]
    acc = x1_ref[...]
    for kk in range(TOP_K):
        acc = acc + gt[:, kk:kk + 1] * ybuf_ref[slot, kk].reshape(td, D_MODEL)
    o_ref[...] = _rms(acc, gfin_ref[...])


def _combine(dest, gates, x1, gfin, y_rows):
    n_tok = x1.shape[0]
    td = min(TD_ROWS, n_tok)
    n_steps = n_tok // td
    dest3 = dest.reshape(n_steps, 1, td * TOP_K)
    return pl.pallas_call(
        _combine_body,
        grid=(n_steps,),
        in_specs=[
            pl.BlockSpec((1, 1, td * TOP_K), lambda i: (i, 0, 0), memory_space=pltpu.SMEM),
            pl.BlockSpec((1, 1, td * TOP_K), lambda i: (jnp.minimum(i + 1, n_steps - 1), 0, 0),
                         memory_space=pltpu.SMEM),
            pl.BlockSpec((td, 2 * TOP_K), lambda i: (i, 0)),
            pl.BlockSpec((td, D_MODEL), lambda i: (i, 0)),
            pl.BlockSpec((1, D_MODEL), lambda i: (0, 0)),
            pl.BlockSpec(memory_space=pl.ANY),
        ],
        out_specs=pl.BlockSpec((td, D_MODEL), lambda i: (i, 0)),
        out_shape=jax.ShapeDtypeStruct((n_tok, D_MODEL), F32),
        scratch_shapes=[pltpu.VMEM((2, TOP_K, td, SUBLANES, LANES), F32),
                        pltpu.SemaphoreType.DMA((2,))],
        compiler_params=pltpu.CompilerParams(
            dimension_semantics=("arbitrary",), vmem_limit_bytes=VMEM_LIMIT),
        name="combine",
    )(dest3, dest3, gates, x1, gfin, y_rows)


def _block_diag(p):
    g, a, b = p.shape
    eye = jnp.eye(g, dtype=p.dtype)
    return (p[:, :, None, :] * eye[:, None, :, None]).reshape(g * a, g * b)


def _retention_tables():
    log_g = jnp.log1p(-(2.0 ** (-5.0 - jnp.arange(RET_HEADS, dtype=F32))))
    pos = jnp.arange(RET_CHUNK, dtype=F32)
    rel = pos[:, None] - pos[None, :]
    intra = jnp.where(rel >= 0, jnp.exp(log_g[:, None, None] * jnp.maximum(rel, 0.0)), 0.0)
    q_decay = jnp.exp(log_g[:, None] * (pos + 1.0))
    k_decay = jnp.exp(log_g[:, None] * (RET_CHUNK - 1.0 - pos))
    chunk_decay = jnp.exp(log_g * RET_CHUNK)
    qd = jnp.broadcast_to(q_decay[:, :, None], (RET_HEADS, RET_CHUNK, RET_DK))
    kd = jnp.broadcast_to(k_decay[:, :, None], (RET_HEADS, RET_CHUNK, RET_DK))
    cd = jnp.broadcast_to(chunk_decay[:, None, None], (RET_HEADS, 1, RET_DV))
    return intra, qd, kd, cd


def _layer(x, positions, norm_mix_g, w_in, lam_re, lam_im, log_dt, b_re, b_im, c_re, c_im,
           s5_d, w_glu, b_glu, ret_gn_g, w_branch_s5, w_branch_ret, w_out, norm_ffn_g,
           router_w, router_b, w1, b1, w2, b2, g_out):
    bsz, seqlen, _ = x.shape
    n_tok = bsz * seqlen

    half = RET_DK // 2
    inv_freq = ROPE_BASE ** (-jnp.arange(half, dtype=F32) / half)
    rope = jnp.stack([jnp.concatenate([inv_freq, inv_freq]),
                      jnp.concatenate([-jnp.ones((half,), F32), jnp.ones((half,), F32)])])
    pos128 = jnp.broadcast_to(positions.astype(F32)[..., None], (bsz, seqlen, LANES))
    u, q, k, v, g_ret, gate_a, gate_b = _inproj(
        x, pos128, rope, norm_mix_g.reshape(1, D_MODEL), w_in.astype(BF16))

    lam = jnp.stack([lam_re.reshape(-1), lam_im.reshape(-1),
                     jnp.repeat(log_dt, S5_STATE)]).astype(F32)
    gpc = S5_GROUPS // S5_CHUNKS

    def chunked_block_diag(p):
        return jax.vmap(_block_diag)(p.reshape(S5_CHUNKS, gpc, p.shape[1], p.shape[2]))

    bblk = jnp.stack([chunked_block_diag(jnp.swapaxes(b_re, 1, 2)),
                      chunked_block_diag(jnp.swapaxes(b_im, 1, 2))])
    cblk = jnp.stack([chunked_block_diag(jnp.swapaxes(c_re, 1, 2)),
                      chunked_block_diag(jnp.swapaxes(c_im, 1, 2))]).astype(BF16)
    dvec = jnp.stack([s5_d, b_glu])
    ys5 = _s5(u, lam, bblk, cblk, dvec, w_glu.astype(BF16), bsz)

    intra, qd, kd, cd = _retention_tables()
    retg, w1p, w2b = _retention(q, k, v, g_ret, intra, qd, kd, cd,
                                ret_gn_g.reshape(1, RET_V_WIDTH), w1, w2)

    rw_hi = router_w.astype(BF16)
    rw_lo = (router_w - rw_hi.astype(F32)).astype(BF16)
    rw = (jnp.zeros((D_MODEL, 2 * LANES), BF16).at[:, :N_EXPERTS].set(rw_hi)
          .at[:, LANES:LANES + N_EXPERTS].set(rw_lo))
    rb = jnp.zeros((1, LANES), F32).at[0, :N_EXPERTS].set(router_b)
    x1, route, gates, counts, x_rows = _merge(
        x, ys5, retg, gate_a, gate_b, w_branch_s5.astype(BF16), w_branch_ret.astype(BF16),
        w_out.astype(BF16), norm_ffn_g.reshape(1, D_MODEL), rw, rb)

    bm = BM_EXPERT
    cap = n_tok
    n_blocks = n_tok * TOP_K // bm + N_EXPERTS
    dest = route.reshape(n_tok, 2 * TOP_K)[:, :TOP_K]
    cnt = counts[0, :N_EXPERTS].astype(jnp.int32)
    nblk = (cnt + bm - 1) // bm
    bend = jnp.cumsum(nblk)
    n_active = bend[N_EXPERTS - 1:]
    blk = jnp.minimum(jnp.arange(n_blocks, dtype=jnp.int32), n_active[0] - 1)
    block_expert = jnp.sum((bend[None, :] <= blk[:, None]).astype(jnp.int32), axis=1)
    sel = (jnp.arange(N_EXPERTS, dtype=jnp.int32)[None, :] == block_expert[:, None]).astype(jnp.int32)
    j_in = blk - jnp.sum(sel * (bend - nblk)[None, :], axis=1)
    block_row = block_expert * (cap // bm) + j_in
    block_valid = jnp.clip(jnp.sum(sel * cnt[None, :], axis=1) - j_in * bm, 0, bm)

    b1p = jnp.concatenate([b1[:, 0::2], b1[:, 1::2]], axis=-1).reshape(N_EXPERTS, 1, 2 * D_FF)
    y_rows = _experts(block_expert.astype(jnp.int32), block_row.astype(jnp.int32),
                      block_valid.astype(jnp.int32), n_active.astype(jnp.int32), x_rows, w1p, b1p,
                      w2b, b2.reshape(N_EXPERTS, 1, D_MODEL), bm)
    out = _combine(dest, gates.reshape(n_tok, 2 * TOP_K), x1.reshape(n_tok, D_MODEL),
                   g_out.reshape(1, D_MODEL), y_rows)
    return out.reshape(bsz, seqlen, D_MODEL)


def kernel(x, positions, norm_mix_g, w_in, s5_lambda_re, s5_lambda_im, s5_log_dt, s5_b_re, s5_b_im, s5_c_re, s5_c_im, s5_d, s5_w_glu, s5_b_glu, ret_gn_g, w_branch_s5, w_branch_ret, w_out, norm_ffn_g, router_w, router_b, expert_w1, expert_b1, expert_w2, expert_b2, norm_final_g):
    assert norm_mix_g.shape[0] == 1, "single-layer trunk"
    return _layer(x, positions, norm_mix_g[0], w_in[0], s5_lambda_re[0], s5_lambda_im[0],
                  s5_log_dt[0], s5_b_re[0], s5_b_im[0], s5_c_re[0], s5_c_im[0], s5_d[0],
                  s5_w_glu[0], s5_b_glu[0], ret_gn_g[0], w_branch_s5[0], w_branch_ret[0],
                  w_out[0], norm_ffn_g[0], router_w[0], router_b[0], expert_w1[0], expert_b1[0],
                  expert_w2[0], expert_b2[0], norm_final_g)
```
